```python
import jax
import jax.numpy as jnp
from jax import lax
import numpy as np

D_MODEL = 2048
BATCH = 4
SEQ = 2048
DEPTH = 4
DEC_BATCH = 128
DEC_SEQ = 8
PAST_LEN = 16384
PAGE_SIZE = 128

MIX_WIDTH = D_MODEL
GLA_WIDTH = MIX_WIDTH // 2
RWKV_WIDTH = MIX_WIDTH - GLA_WIDTH
GLA_HEADS = 4
GLA_KEY = GLA_WIDTH // 2
GLA_HEAD_K = GLA_KEY // GLA_HEADS
GLA_HEAD_V = GLA_WIDTH // GLA_HEADS
GLA_GATE_RANK = 16
GLA_GATE_NORMALIZER = 16.0
GLA_CHUNK = 64
GLA_NORM_EPS = 1e-5
RWKV_HEAD = 64
RWKV_HEADS = RWKV_WIDTH // RWKV_HEAD
RWKV_W_LORA = max(32, int(round(1.8 * RWKV_WIDTH ** 0.5 / 32)) * 32)
RWKV_A_LORA = max(32, int(round(1.8 * RWKV_WIDTH ** 0.5 / 32)) * 32)
RWKV_G_LORA = max(32, int(round(0.6 * RWKV_WIDTH ** 0.8 / 32)) * 32)
RWKV_GN_EPS = 64e-5
GLA_Q0 = 0
GLA_K0 = GLA_Q0 + GLA_KEY
GLA_V0 = GLA_K0 + GLA_KEY
GLA_A0 = GLA_V0 + GLA_WIDTH
GLA_G0 = GLA_A0 + GLA_GATE_RANK
GLA_IN = GLA_G0 + GLA_WIDTH
RW_R0 = 0
RW_K0 = RW_R0 + RWKV_WIDTH
RW_V0 = RW_K0 + RWKV_WIDTH
RW_W0 = RW_V0 + RWKV_WIDTH
RW_A0 = RW_W0 + RWKV_W_LORA
RW_G0 = RW_A0 + RWKV_A_LORA
RWKV_IN = RW_G0 + RWKV_G_LORA
IN_WIDTH = GLA_IN + RWKV_IN
D_FF = 5632
N_MOD = 9
NORM_EPS = 1e-6

kernel_name = 'hymba_gla_rwkv7_macaron_adaln_step'


def _rms_norm(x, g, eps=NORM_EPS):
    xf = x.astype(jnp.float32)
    y = xf * lax.rsqrt(jnp.mean(xf * xf, axis=-1, keepdims=True) + eps)
    return (y * g.astype(jnp.float32)).astype(x.dtype)


def _modulate(x, g, shift, scale):
    return _rms_norm(x, g) * (1 + scale[:, None, :]) + shift[:, None, :]


def _swiglu(h, w_in, w_o):
    u = h @ w_in
    return (jax.nn.silu(u[..., :D_FF]) * u[..., D_FF:]) @ w_o


def _gla_chunked(q, k, v, log_a, s0):
    bsz, t = q.shape[0], q.shape[1]
    c = min(GLA_CHUNK, t)
    pad = (-t) % c
    n = (t + pad) // c

    def to_chunks(a):
        a = jnp.pad(a, ((0, 0), (0, pad), (0, 0), (0, 0)))
        return a.reshape(bsz, n, c, a.shape[2], a.shape[3]).transpose(1, 0, 3, 2, 4)

    causal = jnp.tril(jnp.ones((c, c), dtype=bool))[None, None, :, :, None]

    def step(s, inp):
        qc, kc, vc, gc = inp
        b = jnp.cumsum(gc, axis=2)
        o_inter = jnp.einsum('bhid,bhdv->bhiv', qc * jnp.exp(b), s)
        diff = b[:, :, :, None, :] - b[:, :, None, :, :]
        decay = jnp.where(causal, jnp.exp(jnp.where(causal, diff, 0.0)), 0.0)
        scores = jnp.einsum('bhijd,bhjd->bhij', qc[:, :, :, None, :] * decay, kc)
        o = o_inter + jnp.einsum('bhij,bhjv->bhiv', scores, vc)
        b_last = b[:, :, -1:, :]
        s = s * jnp.exp(b_last[:, :, 0, :, None]) + jnp.einsum('bhjd,bhjv->bhdv', kc * jnp.exp(b_last - b), vc)
        return s, o

    s, o = lax.scan(step, s0, (to_chunks(q), to_chunks(k), to_chunks(v), to_chunks(log_a)))
    o = o.transpose(1, 0, 3, 2, 4).reshape(bsz, n * c, o.shape[2], o.shape[4])[:, :t]
    return o, s


def _wkv7(r, w, k, v, kk, a, s0):
    xs = tuple(z.transpose(1, 0, 2, 3) for z in (r, w, k, v, kk, a))

    def step(s, inp):
        r_t, w_t, k_t, v_t, kk_t, a_t = inp
        sa = jnp.einsum('bhvk,bhk->bhv', s, -kk_t)
        s = (s * w_t[:, :, None, :] + sa[..., None] * (kk_t * a_t)[:, :, None, :]
             + v_t[..., None] * k_t[:, :, None, :])
        return s, jnp.einsum('bhvk,bhk->bhv', s, r_t)

    s, o = lax.scan(step, s0, xs)
    return o.transpose(1, 0, 2, 3), s


def _group_norm_heads(o, w, b):
    mean = jnp.mean(o, axis=-1, keepdims=True)
    var = jnp.mean(jnp.square(o - mean), axis=-1, keepdims=True)
    y = (o - mean) * lax.rsqrt(var + RWKV_GN_EPS)
    return y * w.astype(jnp.float32).reshape(RWKV_HEADS, RWKV_HEAD) + b.astype(jnp.float32).reshape(RWKV_HEADS, RWKV_HEAD)


def _mixer(h, shift_prev, s_gla, s_wkv, w_in, gla_gk_up, gla_gk_b, gla_norm_g, rwkv_mu, rwkv_w0,
           rwkv_w2, rwkv_a0, rwkv_a2, rwkv_g2, rwkv_k_k, rwkv_k_a, rwkv_r_k, rwkv_ln_w, rwkv_ln_b, w_out):
    f32 = jnp.float32
    bsz, t, _ = h.shape
    proj = h @ w_in
    pg, pr = proj[..., :GLA_IN], proj[..., GLA_IN:]
    pr_prev = jnp.concatenate([shift_prev[:, None, :].astype(pr.dtype), pr[:, :-1]], axis=1)
    xr = (pr + (pr_prev - pr) * rwkv_mu).astype(f32)

    pg = pg.astype(f32)
    gla_heads_k = lambda z: z.reshape(bsz, t, GLA_HEADS, GLA_HEAD_K)
    gla_heads_v = lambda z: z.reshape(bsz, t, GLA_HEADS, GLA_HEAD_V)
    q = gla_heads_k(pg[..., GLA_Q0:GLA_K0] * GLA_HEAD_K ** -0.5)
    k = gla_heads_k(pg[..., GLA_K0:GLA_V0])
    v = gla_heads_v(pg[..., GLA_V0:GLA_A0])
    log_a = jax.nn.log_sigmoid(pg[..., GLA_A0:GLA_G0] @ gla_gk_up + gla_gk_b).astype(f32) / GLA_GATE_NORMALIZER
    o_gla, s_gla_new = _gla_chunked(q, k, v, gla_heads_k(log_a), s_gla.astype(f32))
    o_gla = _rms_norm(o_gla, gla_norm_g, GLA_NORM_EPS) * jax.nn.silu(gla_heads_v(pg[..., GLA_G0:GLA_IN]))
    o_gla = o_gla.reshape(bsz, t, GLA_WIDTH).astype(h.dtype)

    heads = lambda z: z.reshape(bsz, t, RWKV_HEADS, RWKV_HEAD)
    r = xr[..., RW_R0:RW_K0]
    kr = xr[..., RW_K0:RW_V0]
    vr = xr[..., RW_V0:RW_W0]
    w_inner = -jax.nn.softplus(-(rwkv_w0 + jnp.tanh(xr[..., RW_W0:RW_A0]) @ rwkv_w2)) - 0.5
    decay = jnp.exp(-jnp.exp(w_inner.astype(f32)))
    a = jax.nn.sigmoid(rwkv_a0 + xr[..., RW_A0:RW_G0] @ rwkv_a2).astype(f32)
    gate = (jax.nn.sigmoid(xr[..., RW_G0:RWKV_IN]) @ rwkv_g2).astype(f32)
    kk = heads(kr * rwkv_k_k).astype(f32)
    kk = kk / jnp.maximum(jnp.sqrt(jnp.sum(kk * kk, axis=-1, keepdims=True)), 1e-12)
    kr = (kr * (1 + (a - 1) * rwkv_k_a)).astype(f32)
    r_h, k_h, v_h = heads(r), heads(kr), heads(vr)
    o_w, s_wkv_new = _wkv7(r_h, heads(decay), k_h, v_h, kk, heads(a), s_wkv.astype(f32))
    o_w = _group_norm_heads(o_w, rwkv_ln_w, rwkv_ln_b)
    o_w = o_w + jnp.sum(r_h * k_h * rwkv_r_k.astype(f32), axis=-1, keepdims=True) * v_h
    o_w = (o_w.reshape(bsz, t, RWKV_WIDTH) * gate).astype(h.dtype)

    y = jnp.concatenate([o_gla, o_w], axis=-1) @ w_out
    return y, pr[:, -1], s_gla_new, s_wkv_new


def _layer(x, c, shift_prev, s_gla, s_wkv, w_ada, b_ada, g_ffn1, w_ffn1_in, w_ffn1_out, g_mix, w_in,
           gla_gk_up, gla_gk_b, gla_norm_g, rwkv_mu, rwkv_w0, rwkv_w2, rwkv_a0, rwkv_a2, rwkv_g2,
           rwkv_k_k, rwkv_k_a, rwkv_r_k, rwkv_ln_w, rwkv_ln_b, w_out, g_ffn2, w_ffn2_in, w_ffn2_out):
    mod = (jax.nn.silu(c) @ w_ada + b_ada).reshape(c.shape[0], N_MOD, D_MODEL)
    h = _modulate(x, g_ffn1, mod[:, 0], mod[:, 1])
    x = x + 0.5 * mod[:, 2][:, None, :] * _swiglu(h, w_ffn1_in, w_ffn1_out)
    h = _modulate(x, g_mix, mod[:, 3], mod[:, 4])
    y, shift_new, s_gla_new, s_wkv_new = _mixer(
        h, shift_prev, s_gla, s_wkv, w_in, gla_gk_up, gla_gk_b, gla_norm_g, rwkv_mu, rwkv_w0, rwkv_w2,
        rwkv_a0, rwkv_a2, rwkv_g2, rwkv_k_k, rwkv_k_a, rwkv_r_k, rwkv_ln_w, rwkv_ln_b, w_out)
    x = x + mod[:, 5][:, None, :] * y
    h = _modulate(x, g_ffn2, mod[:, 6], mod[:, 7])
    x = x + 0.5 * mod[:, 8][:, None, :] * _swiglu(h, w_ffn2_in, w_ffn2_out)
    return x, shift_new, s_gla_new, s_wkv_new


def _trunk(x, c, s_gla, s_wkv, s_shift, layer_weights, g_final):
    new_gla, new_wkv, new_shift = [], [], []
    for l in range(DEPTH):
        x, sh, sg, sw = _layer(x, c, s_shift[l], s_gla[l], s_wkv[l], *[w[l] for w in layer_weights])
        new_gla.append(sg)
        new_wkv.append(sw)
        new_shift.append(sh)
    y = _rms_norm(x, g_final)
    dt = x.dtype
    return y, jnp.stack(new_gla).astype(dt), jnp.stack(new_wkv).astype(dt), jnp.stack(new_shift).astype(dt)


def setup_inputs(seed: int = 0) -> dict:
    key = jax.random.key(seed)
    ks = iter(jax.random.split(key, 40))
    f32 = jnp.float32
    nrm = lambda shape, scale: jax.random.normal(next(ks), shape, f32) * scale
    gain = lambda shape: 1.0 + 0.02 * jax.random.normal(next(ks), shape, f32)
    L, D = DEPTH, D_MODEL
    return {
        'x_prompt': nrm((BATCH, SEQ, D), 1.0),
        'x_sample': nrm((DEC_BATCH, DEC_SEQ, D), 1.0),
        'c_prompt': nrm((BATCH, D), 1.0),
        'c_sample': nrm((DEC_BATCH, D), 1.0),
        'state_gla': nrm((L, DEC_BATCH, GLA_HEADS, GLA_HEAD_K, GLA_HEAD_V), 1.0),
        'state_wkv': nrm((L, DEC_BATCH, RWKV_HEADS, RWKV_HEAD, RWKV_HEAD), 0.5),
        'state_shift': nrm((L, DEC_BATCH, RWKV_IN), 1.0),
        'w_ada': nrm((L, D, N_MOD * D), 0.5 * D ** -0.5),
        'b_ada': nrm((L, N_MOD * D), 0.01),
        'g_ffn1': gain((L, D)),
        'w_ffn1_in': nrm((L, D, 2 * D_FF), D ** -0.5),
        'w_ffn1_out': nrm((L, D_FF, D), D_FF ** -0.5),
        'g_mix': gain((L, D)),
        'w_in': nrm((L, D, IN_WIDTH), D ** -0.5),
        'gla_gk_up': nrm((L, GLA_GATE_RANK, GLA_KEY), GLA_GATE_RANK ** -0.5),
        'gla_gk_b': nrm((L, GLA_KEY), 0.1),
        'gla_norm_g': gain((L, GLA_HEAD_V)),
        'rwkv_mu': jax.random.uniform(next(ks), (L, RWKV_IN), f32, 0.0, 1.0),
        'rwkv_w0': jax.random.uniform(next(ks), (L, RWKV_WIDTH), f32, -4.0, 1.0),
        'rwkv_w2': nrm((L, RWKV_W_LORA, RWKV_WIDTH), 0.1),
        'rwkv_a0': nrm((L, RWKV_WIDTH), 0.1),
        'rwkv_a2': nrm((L, RWKV_A_LORA, RWKV_WIDTH), 0.1),
        'rwkv_g2': nrm((L, RWKV_G_LORA, RWKV_WIDTH), RWKV_G_LORA ** -0.5),
        'rwkv_k_k': 0.85 + 0.02 * jax.random.normal(next(ks), (L, RWKV_WIDTH), f32),
        'rwkv_k_a': gain((L, RWKV_WIDTH)),
        'rwkv_r_k': nrm((L, RWKV_HEADS, RWKV_HEAD), 0.1),
        'rwkv_ln_w': gain((L, RWKV_WIDTH)),
        'rwkv_ln_b': nrm((L, RWKV_WIDTH), 0.01),
        'w_out': nrm((L, MIX_WIDTH, D), MIX_WIDTH ** -0.5),
        'g_ffn2': gain((L, D)),
        'w_ffn2_in': nrm((L, D, 2 * D_FF), D ** -0.5),
        'w_ffn2_out': nrm((L, D_FF, D), D_FF ** -0.5),
        'g_final': gain((D,)),
    }


def reference(x_prompt, x_sample, c_prompt, c_sample, state_gla, state_wkv, state_shift, w_ada, b_ada,
              g_ffn1, w_ffn1_in, w_ffn1_out, g_mix, w_in, gla_gk_up, gla_gk_b, gla_norm_g, rwkv_mu,
              rwkv_w0, rwkv_w2, rwkv_a0, rwkv_a2, rwkv_g2, rwkv_k_k, rwkv_k_a, rwkv_r_k, rwkv_ln_w,
              rwkv_ln_b, w_out, g_ffn2, w_ffn2_in, w_ffn2_out, g_final):
    layer_weights = (w_ada, b_ada, g_ffn1, w_ffn1_in, w_ffn1_out, g_mix, w_in, gla_gk_up, gla_gk_b,
                     gla_norm_g, rwkv_mu, rwkv_w0, rwkv_w2, rwkv_a0, rwkv_a2, rwkv_g2, rwkv_k_k,
                     rwkv_k_a, rwkv_r_k, rwkv_ln_w, rwkv_ln_b, w_out, g_ffn2, w_ffn2_in, w_ffn2_out)
    bp = x_prompt.shape[0]
    zg = jnp.zeros((DEPTH, bp, GLA_HEADS, GLA_HEAD_K, GLA_HEAD_V), jnp.float32)
    zw = jnp.zeros((DEPTH, bp, RWKV_HEADS, RWKV_HEAD, RWKV_HEAD), jnp.float32)
    zs = jnp.zeros((DEPTH, bp, RWKV_IN), x_prompt.dtype)
    y_prompt, gla_p, wkv_p, shift_p = _trunk(x_prompt, c_prompt, zg, zw, zs, layer_weights, g_final)
    y_sample, gla_s, wkv_s, shift_s = _trunk(x_sample, c_sample, state_gla, state_wkv, state_shift,
                                             layer_weights, g_final)
    return (y_prompt, y_sample, gla_p, wkv_p, shift_p, gla_s, wkv_s, shift_s)
```

```python
import functools

import jax
import jax.numpy as jnp
from jax import lax
from jax.experimental import pallas as pl
from jax.experimental.pallas import tpu as pltpu

F32 = jnp.float32
BF16 = jnp.bfloat16

D_MODEL = 2048
D_FF = 5632
N_MOD = 9
NORM_EPS = 1e-6
GLA_HEADS = 4
GLA_HEAD_K = 128
GLA_HEAD_V = 256
GLA_KEY = GLA_HEADS * GLA_HEAD_K
GLA_WIDTH = GLA_HEADS * GLA_HEAD_V
GLA_GATE_RANK = 16
GLA_GATE_NORMALIZER = 16.0
GLA_NORM_EPS = 1e-5
GLA_IN = 2 * GLA_KEY + 2 * GLA_WIDTH + GLA_GATE_RANK
RWKV_HEAD = 64
RWKV_HEADS = 16
RWKV_WIDTH = RWKV_HEADS * RWKV_HEAD
RWKV_W_LORA = 64
RWKV_A_LORA = 64
RWKV_G_LORA = 160
RWKV_IN = 3 * RWKV_WIDTH + RWKV_W_LORA + RWKV_A_LORA + RWKV_G_LORA
RWKV_GN_EPS = 64e-5

LANES = 128
SUBLANES = 8
VMEM_LIMIT_BYTES = 56 * 1024 * 1024

SEG = 3456
PG_Q0, PG_K0, PG_V0, PG_G0, PG_A0 = 0, 512, 1024, 2048, 3072
PR_R0, PR_K0, PR_V0, PR_WA0, PR_G0 = 0, 1024, 2048, 3072, 3200
PR_GW = SEG - PR_G0
N_PAIR = RWKV_HEADS // 2

ROW_TILE = 512
GLA_CHUNK = 64
GLA_SUB = 16
RWKV_CHUNK = 64


def _dot(a, b):
    return jnp.dot(a.astype(BF16), b.astype(BF16), preferred_element_type=F32)


def _dot_nt(a, b):
    return lax.dot_general(a.astype(BF16), b.astype(BF16), (((1,), (1,)), ((), ())),
                           preferred_element_type=F32)


def _dot_tn(a, b):
    return lax.dot_general(a.astype(BF16), b.astype(BF16), (((0,), (0,)), ((), ())),
                           preferred_element_type=F32)


def _split3(x):
    hi = x.astype(BF16)
    r1 = x - hi.astype(F32)
    mid = r1.astype(BF16)
    lo = (r1 - mid.astype(F32)).astype(BF16)
    return hi, mid, lo


def _dot01(m01, x):
    m = m01.astype(BF16)
    hi, mid, lo = _split3(x)
    f = lambda t: jnp.dot(m, t, preferred_element_type=F32)
    return f(hi) + f(mid) + f(lo)


def _dot01_tn(x, m01):
    m = m01.astype(BF16)
    hi, mid, lo = _split3(x)
    f = lambda t: lax.dot_general(t, m, (((0,), (0,)), ((), ())), preferred_element_type=F32)
    return f(hi) + f(mid) + f(lo)


def _segsum(x, ones_blk):
    hi = x.astype(BF16)
    lo = (x - hi.astype(F32)).astype(BF16)
    return (jnp.dot(hi, ones_blk, preferred_element_type=F32)
            + jnp.dot(lo, ones_blk, preferred_element_type=F32))


def _iota(shape, dim):
    return lax.broadcasted_iota(jnp.int32, shape, dim)


def _softplus(z):
    return jnp.maximum(z, 0.0) + jnp.log(1.0 + jnp.exp(-jnp.abs(z)))


def _modulated_norm(x, g, shift, scale):
    y = x * lax.rsqrt(jnp.mean(x * x, axis=-1, keepdims=True) + NORM_EPS) * g
    return y * (1.0 + scale) + shift


def _adaln_kernel(c_ref, w_ref, b_ref, o_ref):
    c = c_ref[...]
    o_ref[...] = _dot(c * jax.nn.sigmoid(c), w_ref[...]) + b_ref[...]


def _adaln(c_all, w_ada, b_ada):
    depth = w_ada.shape[0]
    rows = c_all.shape[0]
    tn = 1024
    nn = D_MODEL // tn
    return pl.pallas_call(
        _adaln_kernel,
        grid=(depth, N_MOD, nn),
        in_specs=[
            pl.BlockSpec((rows, D_MODEL), lambda l, m, n: (0, 0)),
            pl.BlockSpec((None, D_MODEL, tn), lambda l, m, n: (l, 0, m * nn + n)),
            pl.BlockSpec((None, 1, tn), lambda l, m, n: (l, 0, m * nn + n)),
        ],
        out_specs=pl.BlockSpec((None, None, rows, tn), lambda l, m, n: (l, m, 0, n)),
        out_shape=jax.ShapeDtypeStruct((depth, N_MOD, rows, D_MODEL), F32),
        compiler_params=pltpu.CompilerParams(
            dimension_semantics=("parallel", "parallel", "parallel"),
            vmem_limit_bytes=VMEM_LIMIT_BYTES),
        name="adaln",
    )(c_all, w_ada, b_ada.reshape(depth, 1, N_MOD * D_MODEL))


def _row_tiling(bsz, t):
    tb = min(t, ROW_TILE)
    assert t % tb == 0 and tb % SUBLANES == 0
    sb = 1 if tb > SUBLANES else min(bsz, ROW_TILE // tb)
    assert bsz % sb == 0
    return sb, tb, t // tb


def _mod_spec(sb, n_t, layer, m, width=D_MODEL, col=lambda j: 0):
    if sb == 1:
        return pl.BlockSpec((None, None, 1, 1, width), lambda i, j: (layer, m, i // n_t, 0, col(j)))
    return pl.BlockSpec((None, None, sb, width), lambda i, j: (layer, m, i, col(j)))


def _seq_rows(m_ref, sb, tb):
    if sb == 1:
        return m_ref[0]
    rows = sb * tb
    seq, tok = _iota((rows, sb), 1), _iota((rows, sb), 0)
    expand = jnp.where(jnp.logical_and(tok >= seq * tb, tok < (seq + 1) * tb), 1.0, 0.0)
    return _dot01(expand, m_ref[...])


def _ffn_kernel(x_ref, shift_ref, scale_ref, gate_ref, g_ref, w1a_ref, w1b_ref, w2_ref, gfin_ref,
                o_ref, h_ref, acc_ref, *, final_norm):
    j = pl.program_id(1)
    sb, tb, d = x_ref.shape

    @pl.when(j == 0)
    def _():
        h = _modulated_norm(x_ref[...].reshape(sb * tb, d), g_ref[...],
                            _seq_rows(shift_ref, sb, tb), _seq_rows(scale_ref, sb, tb))
        h_ref[...] = h.astype(BF16)
        acc_ref[...] = jnp.zeros_like(acc_ref)

    h = h_ref[...]
    u1 = jnp.dot(h, w1a_ref[...].astype(BF16), preferred_element_type=F32)
    u2 = jnp.dot(h, w1b_ref[...].astype(BF16), preferred_element_type=F32)
    act = (u1 * jax.nn.sigmoid(u1)) * u2
    acc_ref[...] += _dot(act, w2_ref[...])

    @pl.when(j == pl.num_programs(1) - 1)
    def _():
        y = x_ref[...].reshape(sb * tb, d) + 0.5 * _seq_rows(gate_ref, sb, tb) * acc_ref[...]
        if final_norm:
            y = y * lax.rsqrt(jnp.mean(y * y, axis=-1, keepdims=True) + NORM_EPS) * gfin_ref[...]
        o_ref[...] = y.reshape(sb, tb, d)


def _ffn(x, mods, layer, m0, g, w_in, w_out, g_final, final_norm):
    bsz, t, d = x.shape
    sb, tb, n_t = _row_tiling(bsz, t)
    tf = 256
    nf = D_FF // tf
    x_spec = pl.BlockSpec((sb, tb, d), lambda i, j: (i // n_t, i % n_t, 0))
    return pl.pallas_call(
        functools.partial(_ffn_kernel, final_norm=final_norm),
        grid=((bsz // sb) * n_t, nf),
        in_specs=[
            x_spec,
            _mod_spec(sb, n_t, layer, m0), _mod_spec(sb, n_t, layer, m0 + 1),
            _mod_spec(sb, n_t, layer, m0 + 2),
            pl.BlockSpec((None, 1, d), lambda i, j: (layer, 0, 0)),
            pl.BlockSpec((None, d, tf), lambda i, j: (layer, 0, j)),
            pl.BlockSpec((None, d, tf), lambda i, j: (layer, 0, j + nf)),
            pl.BlockSpec((None, tf, d), lambda i, j: (layer, j, 0)),
            pl.BlockSpec((1, d), lambda i, j: (0, 0)),
        ],
        out_specs=x_spec,
        out_shape=jax.ShapeDtypeStruct(x.shape, F32),
        scratch_shapes=[pltpu.VMEM((sb * tb, d), BF16), pltpu.VMEM((sb * tb, d), F32)],
        compiler_params=pltpu.CompilerParams(
            dimension_semantics=("parallel", "arbitrary"), vmem_limit_bytes=VMEM_LIMIT_BYTES),
        name="ffn",
    )(x, mods, mods, mods, g, w_in, w_in, w_out, g_final)


def _inproj_kernel(x_ref, shift_ref, scale_ref, g_ref, w_ref, o_ref, h_ref):
    sb, tb, d = x_ref.shape

    @pl.when(pl.program_id(1) == 0)
    def _():
        h = _modulated_norm(x_ref[...].reshape(sb * tb, d), g_ref[...],
                            _seq_rows(shift_ref, sb, tb), _seq_rows(scale_ref, sb, tb))
        h_ref[...] = h.astype(BF16)

    o_ref[...] = jnp.dot(h_ref[...], w_ref[...].astype(BF16),
                         preferred_element_type=F32).reshape(o_ref.shape)


def _inproj(x, mods, layer, g, w_in_p):
    bsz, t, d = x.shape
    sb, tb, n_t = _row_tiling(bsz, t)
    n_out = w_in_p.shape[-1]
    tn = 1152
    assert n_out % tn == 0
    return pl.pallas_call(
        _inproj_kernel,
        grid=((bsz // sb) * n_t, n_out // tn),
        in_specs=[
            pl.BlockSpec((sb, tb, d), lambda i, j: (i // n_t, i % n_t, 0)),
            _mod_spec(sb, n_t, layer, 3), _mod_spec(sb, n_t, layer, 4),
            pl.BlockSpec((None, 1, d), lambda i, j: (layer, 0, 0)),
            pl.BlockSpec((None, d, tn), lambda i, j: (layer, 0, j)),
        ],
        out_specs=pl.BlockSpec((sb, tb, tn), lambda i, j: (i // n_t, i % n_t, j)),
        out_shape=jax.ShapeDtypeStruct((bsz, t, n_out), F32),
        scratch_shapes=[pltpu.VMEM((sb * tb, d), BF16)],
        compiler_params=pltpu.CompilerParams(
            dimension_semantics=("parallel", "arbitrary"), vmem_limit_bytes=VMEM_LIMIT_BYTES),
        name="inproj",
    )(x, mods, mods, g, w_in_p)


def _outproj_kernel(x_ref, gate_ref, og_ref, ow_ref, wg_ref, ww_ref, o_ref):
    sb, tb, tn = x_ref.shape
    og = og_ref[...].reshape(sb * tb, GLA_WIDTH)
    ow = ow_ref[...].reshape(sb * tb, RWKV_WIDTH)
    y = (jnp.dot(og, wg_ref[...].astype(BF16), preferred_element_type=F32)
         + jnp.dot(ow, ww_ref[...].astype(BF16), preferred_element_type=F32))
    o_ref[...] = (x_ref[...].reshape(sb * tb, tn) + _seq_rows(gate_ref, sb, tb) * y).reshape(sb, tb, tn)


def _outproj(x, mods, layer, o_gla, o_w, w_out):
    bsz, t, d = x.shape
    sb, tb, n_t = _row_tiling(bsz, t)
    tn = 1024
    x_spec = pl.BlockSpec((sb, tb, tn), lambda i, j: (i // n_t, i % n_t, j))
    o_spec = pl.BlockSpec((sb, tb, GLA_WIDTH), lambda i, j: (i // n_t, i % n_t, 0))
    return pl.pallas_call(
        _outproj_kernel,
        grid=((bsz // sb) * n_t, d // tn),
        in_specs=[
            x_spec,
            _mod_spec(sb, n_t, layer, 5, width=tn, col=lambda j: j),
            o_spec, o_spec,
            pl.BlockSpec((None, GLA_WIDTH, tn), lambda i, j: (layer, 0, j)),
            pl.BlockSpec((None, RWKV_WIDTH, tn), lambda i, j: (layer, 1, j)),
        ],
        out_specs=x_spec,
        out_shape=jax.ShapeDtypeStruct(x.shape, F32),
        compiler_params=pltpu.CompilerParams(
            dimension_semantics=("parallel", "arbitrary"), vmem_limit_bytes=VMEM_LIMIT_BYTES),
        name="outproj",
    )(x, mods, o_gla, o_w, w_out, w_out)


def _gla_kernel(pg_ref, s0_ref, up_ref, gb_ref, ng_ref, o_ref, s_ref, *, zero_init, sub):
    c_idx = pl.program_id(1)
    c = pg_ref.shape[1]
    n_sub = c // sub

    @pl.when(c_idx == 0)
    def _():
        if zero_init:
            s_ref[...] = jnp.zeros_like(s_ref)
        else:
            s_ref[...] = s0_ref[...]

    x_a = pg_ref[0, :, PG_A0:PG_A0 + LANES]
    z = _dot(x_a, up_ref[...]) + gb_ref[...]
    glog = -_softplus(-z) / GLA_GATE_NORMALIZER
    tril = (_iota((c, c), 1) <= _iota((c, c), 0)).astype(F32)
    b_all = _dot01(tril, glog)
    ones_cv = jnp.ones((c, GLA_HEAD_V), F32)
    row_i = _iota((sub, LANES), 0)
    lane_ss = _iota((sub, sub), 1)

    for h in range(GLA_HEADS):
        ks = slice(h * GLA_HEAD_K, (h + 1) * GLA_HEAD_K)
        q = pg_ref[0, :, PG_Q0 + h * GLA_HEAD_K:PG_Q0 + (h + 1) * GLA_HEAD_K] * (GLA_HEAD_K ** -0.5)
        k = pg_ref[0, :, PG_K0 + h * GLA_HEAD_K:PG_K0 + (h + 1) * GLA_HEAD_K]
        v = pg_ref[0, :, PG_V0 + h * GLA_HEAD_V:PG_V0 + (h + 1) * GLA_HEAD_V]
        gate = pg_ref[0, :, PG_G0 + h * GLA_HEAD_V:PG_G0 + (h + 1) * GLA_HEAD_V]
        b = b_all[:, ks]
        s0 = s_ref[0, h]

        o_inter = _dot(q * jnp.exp(b), s0)
        o_parts = []
        for i_sub in range(n_sub):
            r0 = i_sub * sub
            rows = slice(r0, r0 + sub)
            q_i, b_i = q[rows], b[rows]
            diag = jnp.zeros((sub, sub), F32)
            for jj in range(sub):
                keep = row_i >= jj
                dec = jnp.where(keep, jnp.exp(jnp.where(keep, b_i - b[r0 + jj:r0 + jj + 1], 0.0)), 0.0)
                col = jnp.sum(q_i * dec * k[r0 + jj:r0 + jj + 1], axis=-1, keepdims=True)
                diag = jnp.where(lane_ss == jj, col, diag)
            o_i = _dot(diag, v[rows])
            if i_sub > 0:
                b_start = b[r0 - 1:r0]
                q_rel = q_i * jnp.exp(b_i - b_start)
                k_rel = k[:r0] * jnp.exp(b_start - b[:r0])
                o_i = o_i + _dot(_dot_nt(q_rel, k_rel), v[:r0])
            o_parts.append(o_i)
        o = o_inter + (jnp.concatenate(o_parts, axis=0) if n_sub > 1 else o_parts[0])

        o = o * lax.rsqrt(jnp.mean(o * o, axis=-1, keepdims=True) + GLA_NORM_EPS) * ng_ref[...]
        o = o * (gate * jax.nn.sigmoid(gate))
        o_ref[0, :, h * GLA_HEAD_V:(h + 1) * GLA_HEAD_V] = o.astype(o_ref.dtype)

        b_last = b[c - 1:c]
        k_bar = k * jnp.exp(b_last - b)
        decay_col = jnp.exp(_dot01_tn(glog[:, ks], ones_cv))
        s_ref[0, h] = s0 * decay_col + _dot_tn(k_bar, v)


def _gla(proj, s0, up_p, gk_b, norm_g, layer, chunk, zero_init):
    bsz, t, _ = proj.shape
    n_c = t // chunk
    s_spec = pl.BlockSpec((1, GLA_HEADS, GLA_HEAD_K, GLA_HEAD_V), lambda b, c: (b, 0, 0, 0))
    if zero_init:
        s0_arg = jnp.zeros((1, GLA_HEADS, GLA_HEAD_K, GLA_HEAD_V), F32)
        s0_spec = pl.BlockSpec((1, GLA_HEADS, GLA_HEAD_K, GLA_HEAD_V), lambda b, c: (0, 0, 0, 0))
    else:
        s0_arg = s0
        s0_spec = pl.BlockSpec((None, 1, GLA_HEADS, GLA_HEAD_K, GLA_HEAD_V),
                               lambda b, c: (layer, b, 0, 0, 0))
    return pl.pallas_call(
        functools.partial(_gla_kernel, zero_init=zero_init, sub=min(GLA_SUB, chunk)),
        grid=(bsz, n_c),
        in_specs=[
            pl.BlockSpec((1, chunk, SEG), lambda b, c: (b, c, 0)),
            s0_spec,
            pl.BlockSpec((None, LANES, GLA_KEY), lambda b, c: (layer, 0, 0)),
            pl.BlockSpec((None, 1, GLA_KEY), lambda b, c: (layer, 0, 0)),
            pl.BlockSpec((None, 1, GLA_HEAD_V), lambda b, c: (layer, 0, 0)),
        ],
        out_specs=[
            pl.BlockSpec((1, chunk, GLA_WIDTH), lambda b, c: (b, c, 0)),
            s_spec,
        ],
        out_shape=[
            jax.ShapeDtypeStruct((bsz, t, GLA_WIDTH), BF16),
            jax.ShapeDtypeStruct((bsz, GLA_HEADS, GLA_HEAD_K, GLA_HEAD_V), F32),
        ],
        compiler_params=pltpu.CompilerParams(
            dimension_semantics=("parallel", "arbitrary"), vmem_limit_bytes=VMEM_LIMIT_BYTES),
        name="gla",
    )(proj, s0_arg, up_p, gk_b, norm_g)


def _rwkv_kernel(pr_ref, sh0_ref, s0_ref, mu_ref, w0_ref, w2_ref, a0_ref, a2_ref, g2_ref, kk_ref,
                 ka_ref, rk_ref, lnw_ref, lnb_ref, o_ref, s_ref, sh_ref, *, zero_init):
    c_idx = pl.program_id(1)
    c = pr_ref.shape[1]
    c2 = 2 * c

    @pl.when(c_idx == 0)
    def _():
        sh_ref[...] = sh0_ref[...]
        if zero_init:
            s_ref[...] = jnp.zeros_like(s_ref)
        else:
            s_ref[...] = s0_ref[...]

    pr = pr_ref[0]
    prev_row = sh_ref[0]
    pr_prev = jnp.where(_iota((c, SEG), 0) == 0, prev_row, pltpu.roll(pr, 1, 0))
    sh_ref[0] = pr[c - 1:c]
    xr = pr + (pr_prev - pr) * mu_ref[...]

    r = xr[:, PR_R0:PR_R0 + RWKV_WIDTH]
    kr = xr[:, PR_K0:PR_K0 + RWKV_WIDTH]
    vr = xr[:, PR_V0:PR_V0 + RWKV_WIDTH]
    x_wa = xr[:, PR_WA0:PR_WA0 + LANES]
    x_g = xr[:, PR_G0:PR_G0 + PR_GW]
    w_inner = -_softplus(-(w0_ref[...] + _dot(jnp.tanh(x_wa), w2_ref[...]))) - 0.5
    log_w = -jnp.exp(w_inner)
    a = jax.nn.sigmoid(a0_ref[...] + _dot(x_wa, a2_ref[...]))
    gate = _dot(jax.nn.sigmoid(x_g), g2_ref[...])
    k_mod = kr * (1.0 + (a - 1.0) * ka_ref[...])
    kk_raw = kr * kk_ref[...]

    tril = (_iota((c, c), 1) <= _iota((c, c), 0)).astype(F32)
    cum = _dot01(tril, log_w)
    cum_ex = cum - log_w
    cum_last = cum[c - 1:c]

    lane = _iota((c, LANES), 1)
    head0 = lane < RWKV_HEAD
    ones_blk = ((_iota((LANES, LANES), 0) < RWKV_HEAD) == (_iota((LANES, LANES), 1) < RWKV_HEAD)).astype(BF16)
    row2 = _iota((c2, c2), 0)
    col2 = _iota((c2, c2), 1)
    same_head = (row2 < c) == (col2 < c)
    lower_strict = jnp.logical_and(same_head, col2 < row2)
    lower_incl = jnp.logical_and(same_head, col2 <= row2)
    eye2 = (row2 == col2).astype(F32)
    n_double = max(c.bit_length() - 2, 0)

    def stack(m):
        return jnp.concatenate([jnp.where(head0, m, 0.0), jnp.where(head0, 0.0, m)], axis=0)

    for p in range(N_PAIR):
        sl = slice(p * LANES, (p + 1) * LANES)
        cm, cx = cum[:, sl], cum_ex[:, sl]
        kk = kk_raw[:, sl]
        kk = kk / jnp.maximum(jnp.sqrt(_segsum(kk * kk, ones_blk)), 1e-12)
        a_p, k_p, v_p, r_p = a[:, sl], k_mod[:, sl], vr[:, sl], r[:, sl]
        beta = kk * a_p
        e_neg = jnp.exp(-cm)
        e_last = jnp.exp(cum_last[:, sl] - cm)
        al_s = stack(-kk * jnp.exp(cx))
        rt_s = stack(r_p * jnp.exp(cm))
        bh_s = stack(beta * e_neg)
        kh_s = stack(k_p * e_neg)
        v_s = stack(v_p)
        s0 = s_ref[0, p]

        a_b = jnp.where(lower_strict, _dot_nt(al_s, bh_s), 0.0)
        a_k = jnp.where(lower_strict, _dot_nt(al_s, kh_s), 0.0)
        b_b = jnp.where(lower_incl, _dot_nt(rt_s, bh_s), 0.0)
        b_k = jnp.where(lower_incl, _dot_nt(rt_s, kh_s), 0.0)
        t_inv = eye2 + a_b
        pw = a_b
        for _ in range(n_double):
            pw = _dot(pw, pw)
            t_inv = t_inv + _dot(t_inv, pw)
        u_s = _dot(t_inv, _dot(a_k, v_s) + _dot_nt(al_s, s0))
        o_s = _dot_nt(rt_s, s0) + _dot(b_b, u_s) + _dot(b_k, v_s)
        o_p = o_s[:c] + o_s[c:]

        s_ref[0, p] = (s0 * jnp.exp(cum_last[:, sl])
                       + _dot_tn(u_s, stack(beta * e_last)) + _dot_tn(v_s, stack(k_p * e_last)))

        mean = _segsum(o_p, ones_blk) * (1.0 / RWKV_HEAD)
        cen = o_p - mean
        var = _segsum(cen * cen, ones_blk) * (1.0 / RWKV_HEAD)
        y = cen * lax.rsqrt(var + RWKV_GN_EPS) * lnw_ref[:, sl] + lnb_ref[:, sl]
        y = y + _segsum(r_p * k_p * rk_ref[:, sl], ones_blk) * v_p
        o_ref[0, :, sl] = (y * gate[:, sl]).astype(o_ref.dtype)


def _rwkv(proj, shift0, s0_blk, wts, layer, chunk, zero_init):
    bsz, t, _ = proj.shape
    n_c = t // chunk
    s_spec = pl.BlockSpec((1, N_PAIR, LANES, LANES), lambda b, c: (b, 0, 0, 0))
    if zero_init:
        s0_arg = jnp.zeros((1, N_PAIR, LANES, LANES), F32)
        s0_spec = pl.BlockSpec((1, N_PAIR, LANES, LANES), lambda b, c: (0, 0, 0, 0))
    else:
        s0_arg = s0_blk
        s0_spec = pl.BlockSpec((None, 1, N_PAIR, LANES, LANES), lambda b, c: (layer, b, 0, 0, 0))
    vec = lambda n: pl.BlockSpec((None, 1, n), lambda b, c: (layer, 0, 0))
    mat = lambda m, n: pl.BlockSpec((None, m, n), lambda b, c: (layer, 0, 0))
    return pl.pallas_call(
        functools.partial(_rwkv_kernel, zero_init=zero_init),
        grid=(bsz, n_c),
        in_specs=[
            pl.BlockSpec((1, chunk, SEG), lambda b, c: (b, c, 1)),
            pl.BlockSpec((None, 1, 1, SEG), lambda b, c: (layer, b, 0, 0)),
            s0_spec,
            vec(SEG), vec(RWKV_WIDTH), mat(LANES, RWKV_WIDTH), vec(RWKV_WIDTH),
            mat(LANES, RWKV_WIDTH), mat(PR_GW, RWKV_WIDTH), vec(RWKV_WIDTH), vec(RWKV_WIDTH),
            vec(RWKV_WIDTH), vec(RWKV_WIDTH), vec(RWKV_WIDTH),
        ],
        out_specs=[
            pl.BlockSpec((1, chunk, RWKV_WIDTH), lambda b, c: (b, c, 0)),
            s_spec,
            pl.BlockSpec((1, 1, SEG), lambda b, c: (b, 0, 0)),
        ],
        out_shape=[
            jax.ShapeDtypeStruct((bsz, t, RWKV_WIDTH), BF16),
            jax.ShapeDtypeStruct((bsz, N_PAIR, LANES, LANES), F32),
            jax.ShapeDtypeStruct((bsz, 1, SEG), F32),
        ],
        compiler_params=pltpu.CompilerParams(
            dimension_semantics=("parallel", "arbitrary"), vmem_limit_bytes=VMEM_LIMIT_BYTES),
        name="rwkv",
    )(proj, shift0, s0_arg, *wts)


def _pad_rows(w, rows_before, rows_total):
    depth, r, n = w.shape
    return jnp.concatenate([jnp.zeros((depth, rows_before, n), w.dtype), w,
                            jnp.zeros((depth, rows_total - rows_before - r, n), w.dtype)], axis=1)


def _wkv_to_blockdiag(s):
    depth, bsz = s.shape[:2]
    s = s.reshape(depth, bsz, N_PAIR, 2, RWKV_HEAD, RWKV_HEAD)
    z = jnp.zeros_like(s[:, :, :, 0])
    top = jnp.concatenate([s[:, :, :, 0], z], axis=-1)
    bot = jnp.concatenate([z, s[:, :, :, 1]], axis=-1)
    return jnp.concatenate([top, bot], axis=-2)


def _wkv_from_blockdiag(s):
    bsz = s.shape[0]
    h0 = s[:, :, :RWKV_HEAD, :RWKV_HEAD]
    h1 = s[:, :, RWKV_HEAD:, RWKV_HEAD:]
    return jnp.stack([h0, h1], axis=2).reshape(bsz, RWKV_HEADS, RWKV_HEAD, RWKV_HEAD)


def _trunk(x, mods, state_gla, state_wkv_blk, state_shift_p, params, zero_init):
    (g_ffn1, w_ffn1_in, w_ffn1_out, g_mix, w_in_p, up_p, gk_b, norm_g, rwkv_wts, w_out, g_ffn2,
     w_ffn2_in, w_ffn2_out, g_final) = params
    depth = w_in_p.shape[0]
    bsz, t, _ = x.shape
    gla_chunk = min(GLA_CHUNK, t)
    rwkv_chunk = min(RWKV_CHUNK, t)
    new_gla, new_wkv, new_shift = [], [], []
    for layer in range(depth):
        x = _ffn(x, mods, layer, 0, g_ffn1, w_ffn1_in, w_ffn1_out, g_final, False)
        proj = _inproj(x, mods, layer, g_mix, w_in_p)
        o_gla, s_gla = _gla(proj, state_gla, up_p, gk_b, norm_g, layer, gla_chunk, zero_init)
        o_w, s_wkv, shift = _rwkv(proj, state_shift_p, state_wkv_blk, rwkv_wts, layer, rwkv_chunk,
                                  zero_init)
        x = _outproj(x, mods, layer, o_gla, o_w, w_out)
        x = _ffn(x, mods, layer, 6, g_ffn2, w_ffn2_in, w_ffn2_out, g_final, layer == depth - 1)
        new_gla.append(s_gla)
        new_wkv.append(_wkv_from_blockdiag(s_wkv))
        new_shift.append(shift[:, 0, :RWKV_IN])
    return x, jnp.stack(new_gla), jnp.stack(new_wkv), jnp.stack(new_shift)


def kernel(x_prompt, x_sample, c_prompt, c_sample, state_gla, state_wkv, state_shift, w_ada, b_ada, g_ffn1, w_ffn1_in, w_ffn1_out, g_mix, w_in, gla_gk_up, gla_gk_b, gla_norm_g, rwkv_mu, rwkv_w0, rwkv_w2, rwkv_a0, rwkv_a2, rwkv_g2, rwkv_k_k, rwkv_k_a, rwkv_r_k, rwkv_ln_w, rwkv_ln_b, w_out, g_ffn2, w_ffn2_in, w_ffn2_out, g_final):
    depth = w_in.shape[0]
    bp, bs = x_prompt.shape[0], x_sample.shape[0]
    assert x_prompt.shape[-1] == D_MODEL and w_in.shape[-1] == GLA_IN + RWKV_IN

    n_seq = bp + bs
    n_seq_p = -(-n_seq // SUBLANES) * SUBLANES
    c_all = jnp.concatenate([c_prompt, c_sample, jnp.zeros((n_seq_p - n_seq, D_MODEL), F32)], axis=0)
    mods = _adaln(c_all, w_ada, b_ada)
    mods_p = mods[:, :, :bp, None, :]
    mods_s = mods[:, :, bp:n_seq, :]

    zcol = lambda n: jnp.zeros((depth, D_MODEL, n), w_in.dtype)
    a0 = 2 * GLA_KEY + GLA_WIDTH
    w_in_p = jnp.concatenate([
        w_in[:, :, :a0], w_in[:, :, a0 + GLA_GATE_RANK:GLA_IN], w_in[:, :, a0:a0 + GLA_GATE_RANK],
        zcol(SEG - GLA_IN), w_in[:, :, GLA_IN:], zcol(SEG - RWKV_IN)], axis=-1)
    row = lambda v: v.reshape(depth, 1, -1)
    pad_seg = lambda v: jnp.pad(v, [(0, 0)] * (v.ndim - 1) + [(0, SEG - RWKV_IN)])
    up_p = _pad_rows(gla_gk_up, 0, LANES)
    rwkv_wts = (
        row(pad_seg(rwkv_mu)), row(rwkv_w0), _pad_rows(rwkv_w2, 0, LANES), row(rwkv_a0),
        _pad_rows(rwkv_a2, RWKV_W_LORA, LANES), _pad_rows(rwkv_g2, 0, PR_GW), row(rwkv_k_k),
        row(rwkv_k_a), row(rwkv_r_k), row(rwkv_ln_w), row(rwkv_ln_b))
    params = (row(g_ffn1), w_ffn1_in, w_ffn1_out, row(g_mix), w_in_p, up_p, row(gla_gk_b),
              row(gla_norm_g), rwkv_wts, w_out, row(g_ffn2), w_ffn2_in, w_ffn2_out,
              g_final.reshape(1, D_MODEL))

    shift_zero = jnp.zeros((depth, bp, 1, SEG), F32)
    y_p, gla_p, wkv_p, shift_p = _trunk(x_prompt, mods_p, None, None, shift_zero, params, True)
    shift_s0 = pad_seg(state_shift)[:, :, None, :]
    y_s, gla_s, wkv_s, shift_s = _trunk(x_sample, mods_s, state_gla, _wkv_to_blockdiag(state_wkv),
                                        shift_s0, params, False)
    return (y_p, y_s, gla_p, wkv_p, shift_p, gla_s, wkv_s, shift_s)
```

```python
import functools

import jax
import jax.numpy as jnp
from jax import lax
from jax.experimental import pallas as pl
from jax.experimental.pallas import tpu as pltpu

F32 = jnp.float32
BF16 = jnp.bfloat16

D_MODEL = 2048
D_FF = 5632
N_MOD = 9
NORM_EPS = 1e-6
GLA_HEADS = 4
GLA_HEAD_K = 128
GLA_HEAD_V = 256
GLA_KEY = GLA_HEADS * GLA_HEAD_K
GLA_WIDTH = GLA_HEADS * GLA_HEAD_V
GLA_GATE_RANK = 16
GLA_GATE_NORMALIZER = 16.0
GLA_NORM_EPS = 1e-5
GLA_IN = 2 * GLA_KEY + 2 * GLA_WIDTH + GLA_GATE_RANK
RWKV_HEAD = 64
RWKV_HEADS = 16
RWKV_WIDTH = RWKV_HEADS * RWKV_HEAD
RWKV_W_LORA = 64
RWKV_A_LORA = 64
RWKV_G_LORA = 160
RWKV_IN = 3 * RWKV_WIDTH + RWKV_W_LORA + RWKV_A_LORA + RWKV_G_LORA
RWKV_GN_EPS = 64e-5

LANES = 128
SUBLANES = 8
VMEM_LIMIT_BYTES = 56 * 1024 * 1024

SEG = 3456
PG_Q0, PG_K0, PG_V0, PG_G0, PG_A0 = 0, 512, 1024, 2048, 3072
PR_R0, PR_K0, PR_V0, PR_WA0, PR_G0 = 0, 1024, 2048, 3072, 3200
PR_GW = SEG - PR_G0
N_PAIR = RWKV_HEADS // 2

ROW_TILE = 1024
GLA_CHUNK = 64
GLA_SUB = 16
RWKV_CHUNK = 64
RWKV_SEQ_PER_BLOCK = 4


def _dot(a, b):
    return jnp.dot(a.astype(BF16), b.astype(BF16), preferred_element_type=F32)


def _dot_nt(a, b):
    return lax.dot_general(a.astype(BF16), b.astype(BF16), (((1,), (1,)), ((), ())),
                           preferred_element_type=F32)


def _dot_tn(a, b):
    return lax.dot_general(a.astype(BF16), b.astype(BF16), (((0,), (0,)), ((), ())),
                           preferred_element_type=F32)


def _split3(x):
    hi = x.astype(BF16)
    r1 = x - hi.astype(F32)
    mid = r1.astype(BF16)
    lo = (r1 - mid.astype(F32)).astype(BF16)
    return hi, mid, lo


def _dot01(m01, x):
    m = m01.astype(BF16)
    hi, mid, lo = _split3(x)
    f = lambda t: jnp.dot(m, t, preferred_element_type=F32)
    return f(hi) + f(mid) + f(lo)


def _dot01_tn(x, m01):
    m = m01.astype(BF16)
    hi, mid, lo = _split3(x)
    f = lambda t: lax.dot_general(t, m, (((0,), (0,)), ((), ())), preferred_element_type=F32)
    return f(hi) + f(mid) + f(lo)


def _segsum(x, ones_blk):
    hi = x.astype(BF16)
    lo = (x - hi.astype(F32)).astype(BF16)
    return (jnp.dot(hi, ones_blk, preferred_element_type=F32)
            + jnp.dot(lo, ones_blk, preferred_element_type=F32))


def _iota(shape, dim):
    return lax.broadcasted_iota(jnp.int32, shape, dim)


def _softplus(z):
    return jnp.maximum(z, 0.0) + jnp.log(1.0 + jnp.exp(-jnp.abs(z)))


def _modulated_norm(x, g, shift, scale):
    y = x * lax.rsqrt(jnp.mean(x * x, axis=-1, keepdims=True) + NORM_EPS) * g
    return y * (1.0 + scale) + shift


def _adaln_kernel(c_ref, w_ref, b_ref, o_ref):
    c = c_ref[...]
    o_ref[...] = _dot(c * jax.nn.sigmoid(c), w_ref[...]) + b_ref[...]


def _adaln(c_all, w_ada, b_ada):
    depth = w_ada.shape[0]
    rows = c_all.shape[0]
    tn = 1024
    nn = D_MODEL // tn
    return pl.pallas_call(
        _adaln_kernel,
        grid=(depth, N_MOD, nn),
        in_specs=[
            pl.BlockSpec((rows, D_MODEL), lambda l, m, n: (0, 0)),
            pl.BlockSpec((None, D_MODEL, tn), lambda l, m, n: (l, 0, m * nn + n)),
            pl.BlockSpec((None, 1, tn), lambda l, m, n: (l, 0, m * nn + n)),
        ],
        out_specs=pl.BlockSpec((None, None, rows, tn), lambda l, m, n: (l, m, 0, n)),
        out_shape=jax.ShapeDtypeStruct((depth, N_MOD, rows, D_MODEL), F32),
        compiler_params=pltpu.CompilerParams(
            dimension_semantics=("parallel", "parallel", "parallel"),
            vmem_limit_bytes=VMEM_LIMIT_BYTES),
        name="adaln",
    )(c_all, w_ada, b_ada.reshape(depth, 1, N_MOD * D_MODEL))


def _row_tiling(bsz, t):
    tb = min(t, ROW_TILE)
    assert t % tb == 0 and tb % SUBLANES == 0
    sb = 1 if tb > SUBLANES else min(bsz, ROW_TILE // tb)
    assert bsz % sb == 0
    return sb, tb, t // tb


def _mod_spec(sb, n_t, layer, m, width=D_MODEL, col=lambda j: 0):
    if sb == 1:
        return pl.BlockSpec((None, None, 1, 1, width), lambda i, j: (layer, m, i // n_t, 0, col(j)))
    return pl.BlockSpec((None, None, sb, width), lambda i, j: (layer, m, i, col(j)))


SLAB = 256


def _slabs(sb, tb):
    n = min(SLAB, sb * tb)
    assert (sb * tb) % n == 0 and (sb == 1 or n % tb == 0)
    return [(r0, n) for r0 in range(0, sb * tb, n)]


def _tok_rows(x_ref, sb, tb, r0, n):
    if sb == 1:
        return x_ref[0, r0:r0 + n, :]
    return x_ref[r0 // tb:(r0 + n) // tb].reshape(n, x_ref.shape[-1])


def _put_rows(o_ref, sb, tb, r0, n, val):
    if sb == 1:
        o_ref[0, r0:r0 + n, :] = val
    else:
        o_ref[r0 // tb:(r0 + n) // tb] = val.reshape(n // tb, tb, val.shape[-1])


def _seq_rows(m_ref, sb, tb, r0, n):
    if sb == 1:
        return m_ref[0]
    n_seq = n // tb
    seq, tok = _iota((n, n_seq), 1), _iota((n, n_seq), 0)
    expand = jnp.where(jnp.logical_and(tok >= seq * tb, tok < (seq + 1) * tb), 1.0, 0.0)
    return _dot01(expand, m_ref[r0 // tb:(r0 + n) // tb])


def _write_modulated_norm(x_ref, g_ref, shift_ref, scale_ref, h_ref):
    sb, tb, _ = x_ref.shape
    for r0, n in _slabs(sb, tb):
        h = _modulated_norm(_tok_rows(x_ref, sb, tb, r0, n), g_ref[...],
                            _seq_rows(shift_ref, sb, tb, r0, n), _seq_rows(scale_ref, sb, tb, r0, n))
        h_ref[r0:r0 + n, :] = h.astype(BF16)


def _ffn_kernel(x_ref, shift_ref, scale_ref, gate_ref, g_ref, w1a_ref, w1b_ref, w2_ref, gfin_ref,
                o_ref, h_ref, *, final_norm):
    j = pl.program_id(1)
    sb, tb, d = x_ref.shape

    @pl.when(j == 0)
    def _():
        _write_modulated_norm(x_ref, g_ref, shift_ref, scale_ref, h_ref)
        o_ref[...] = jnp.zeros_like(o_ref)

    h = h_ref[...]
    u1 = jnp.dot(h, w1a_ref[...].astype(BF16), preferred_element_type=F32)
    u2 = jnp.dot(h, w1b_ref[...].astype(BF16), preferred_element_type=F32)
    act = (u1 * jax.nn.sigmoid(u1)) * u2
    o_ref[...] += _dot(act, w2_ref[...]).reshape(sb, tb, d)

    @pl.when(j == pl.num_programs(1) - 1)
    def _():
        for r0, n in _slabs(sb, tb):
            y = (_tok_rows(x_ref, sb, tb, r0, n)
                 + 0.5 * _seq_rows(gate_ref, sb, tb, r0, n) * _tok_rows(o_ref, sb, tb, r0, n))
            if final_norm:
                y = y * lax.rsqrt(jnp.mean(y * y, axis=-1, keepdims=True) + NORM_EPS) * gfin_ref[...]
            _put_rows(o_ref, sb, tb, r0, n, y)


def _ffn(x, mods, layer, m0, g, w_in, w_out, g_final, final_norm):
    bsz, t, d = x.shape
    sb, tb, n_t = _row_tiling(bsz, t)
    tf = 256
    nf = D_FF // tf
    x_map = lambda i, j: (i // n_t, i % n_t, 0)
    x_spec = pl.BlockSpec((sb, tb, d), x_map)
    return pl.pallas_call(
        functools.partial(_ffn_kernel, final_norm=final_norm),
        grid=((bsz // sb) * n_t, nf),
        in_specs=[
            pl.BlockSpec((sb, tb, d), x_map, pipeline_mode=pl.Buffered(1)),
            _mod_spec(sb, n_t, layer, m0), _mod_spec(sb, n_t, layer, m0 + 1),
            _mod_spec(sb, n_t, layer, m0 + 2),
            pl.BlockSpec((None, 1, d), lambda i, j: (layer, 0, 0)),
            pl.BlockSpec((None, d, tf), lambda i, j: (layer, 0, j)),
            pl.BlockSpec((None, d, tf), lambda i, j: (layer, 0, j + nf)),
            pl.BlockSpec((None, tf, d), lambda i, j: (layer, j, 0)),
            pl.BlockSpec((1, d), lambda i, j: (0, 0)),
        ],
        out_specs=x_spec,
        out_shape=jax.ShapeDtypeStruct(x.shape, F32),
        scratch_shapes=[pltpu.VMEM((sb * tb, d), BF16)],
        compiler_params=pltpu.CompilerParams(
            dimension_semantics=("parallel", "arbitrary"), vmem_limit_bytes=VMEM_LIMIT_BYTES),
        name="ffn",
    )(x, mods, mods, mods, g, w_in, w_in, w_out, g_final)


def _inproj_kernel(x_ref, shift_ref, scale_ref, g_ref, w_ref, o_ref, h_ref):
    sb, tb, d = x_ref.shape

    @pl.when(pl.program_id(1) == 0)
    def _():
        _write_modulated_norm(x_ref, g_ref, shift_ref, scale_ref, h_ref)

    o_ref[...] = jnp.dot(h_ref[...], w_ref[...].astype(BF16),
                         preferred_element_type=F32).reshape(o_ref.shape)


def _inproj(x, mods, layer, g, w_in_p):
    bsz, t, d = x.shape
    sb, tb, n_t = _row_tiling(bsz, t)
    n_out = w_in_p.shape[-1]
    tn = 1152
    assert n_out % tn == 0
    return pl.pallas_call(
        _inproj_kernel,
        grid=((bsz // sb) * n_t, n_out // tn),
        in_specs=[
            pl.BlockSpec((sb, tb, d), lambda i, j: (i // n_t, i % n_t, 0),
                         pipeline_mode=pl.Buffered(1)),
            _mod_spec(sb, n_t, layer, 3), _mod_spec(sb, n_t, layer, 4),
            pl.BlockSpec((None, 1, d), lambda i, j: (layer, 0, 0)),
            pl.BlockSpec((None, d, tn), lambda i, j: (layer, 0, j)),
        ],
        out_specs=pl.BlockSpec((sb, tb, tn), lambda i, j: (i // n_t, i % n_t, j)),
        out_shape=jax.ShapeDtypeStruct((bsz, t, n_out), F32),
        scratch_shapes=[pltpu.VMEM((sb * tb, d), BF16)],
        compiler_params=pltpu.CompilerParams(
            dimension_semantics=("parallel", "arbitrary"), vmem_limit_bytes=VMEM_LIMIT_BYTES),
        name="inproj",
    )(x, mods, mods, g, w_in_p)


def _outproj_kernel(x_ref, gate_ref, og_ref, ow_ref, wg_ref, ww_ref, o_ref):
    sb, tb, tn = x_ref.shape
    og = og_ref[...].reshape(sb * tb, GLA_WIDTH)
    ow = ow_ref[...].reshape(sb * tb, RWKV_WIDTH)
    y = (jnp.dot(og, wg_ref[...].astype(BF16), preferred_element_type=F32)
         + jnp.dot(ow, ww_ref[...].astype(BF16), preferred_element_type=F32))
    o_ref[...] = (x_ref[...].reshape(sb * tb, tn)
                  + _seq_rows(gate_ref, sb, tb, 0, sb * tb) * y).reshape(sb, tb, tn)


def _outproj(x, mods, layer, o_gla, o_w, w_out):
    bsz, t, d = x.shape
    sb, tb, n_t = _row_tiling(bsz, t)
    tn = 1024
    x_spec = pl.BlockSpec((sb, tb, tn), lambda i, j: (i // n_t, i % n_t, j))
    o_spec = pl.BlockSpec((sb, tb, GLA_WIDTH), lambda i, j: (i // n_t, i % n_t, 0))
    return pl.pallas_call(
        _outproj_kernel,
        grid=((bsz // sb) * n_t, d // tn),
        in_specs=[
            x_spec,
            _mod_spec(sb, n_t, layer, 5, width=tn, col=lambda j: j),
            o_spec, o_spec,
            pl.BlockSpec((None, GLA_WIDTH, tn), lambda i, j: (layer, 0, j)),
            pl.BlockSpec((None, RWKV_WIDTH, tn), lambda i, j: (layer, 1, j)),
        ],
        out_specs=x_spec,
        out_shape=jax.ShapeDtypeStruct(x.shape, F32),
        compiler_params=pltpu.CompilerParams(
            dimension_semantics=("parallel", "arbitrary"), vmem_limit_bytes=VMEM_LIMIT_BYTES),
        name="outproj",
    )(x, mods, o_gla, o_w, w_out, w_out)


def _gla_kernel(pg_ref, s0_ref, up_ref, gb_ref, ng_ref, o_ref, s_ref, *, zero_init, sub):
    c_idx = pl.program_id(1)
    c = pg_ref.shape[1]
    n_sub = c // sub

    @pl.when(c_idx == 0)
    def _():
        if zero_init:
            s_ref[...] = jnp.zeros_like(s_ref)
        else:
            s_ref[...] = s0_ref[...]

    x_a = pg_ref[0, :, PG_A0:PG_A0 + LANES]
    z = _dot(x_a, up_ref[...]) + gb_ref[...]
    glog = -_softplus(-z) / GLA_GATE_NORMALIZER
    tril = (_iota((c, c), 1) <= _iota((c, c), 0)).astype(F32)
    b_all = _dot01(tril, glog)
    ones_cv = jnp.ones((c, GLA_HEAD_V), F32)
    row_i = _iota((sub, LANES), 0)
    lane_ss = _iota((sub, sub), 1)

    for h in range(GLA_HEADS):
        ks = slice(h * GLA_HEAD_K, (h + 1) * GLA_HEAD_K)
        q = pg_ref[0, :, PG_Q0 + h * GLA_HEAD_K:PG_Q0 + (h + 1) * GLA_HEAD_K] * (GLA_HEAD_K ** -0.5)
        k = pg_ref[0, :, PG_K0 + h * GLA_HEAD_K:PG_K0 + (h + 1) * GLA_HEAD_K]
        v = pg_ref[0, :, PG_V0 + h * GLA_HEAD_V:PG_V0 + (h + 1) * GLA_HEAD_V]
        gate = pg_ref[0, :, PG_G0 + h * GLA_HEAD_V:PG_G0 + (h + 1) * GLA_HEAD_V]
        b = b_all[:, ks]
        s0 = s_ref[0, h]

        o_inter = _dot(q * jnp.exp(b), s0)
        o_parts = []
        for i_sub in range(n_sub):
            r0 = i_sub * sub
            rows = slice(r0, r0 + sub)
            q_i, b_i = q[rows], b[rows]
            diag = jnp.zeros((sub, sub), F32)
            for jj in range(sub):
                keep = row_i >= jj
                dec = jnp.where(keep, jnp.exp(jnp.where(keep, b_i - b[r0 + jj:r0 + jj + 1], 0.0)), 0.0)
                col = jnp.sum(q_i * dec * k[r0 + jj:r0 + jj + 1], axis=-1, keepdims=True)
                diag = jnp.where(lane_ss == jj, col, diag)
            o_i = _dot(diag, v[rows])
            if i_sub > 0:
                b_start = b[r0 - 1:r0]
                q_rel = q_i * jnp.exp(b_i - b_start)
                k_rel = k[:r0] * jnp.exp(b_start - b[:r0])
                o_i = o_i + _dot(_dot_nt(q_rel, k_rel), v[:r0])
            o_parts.append(o_i)
        o = o_inter + (jnp.concatenate(o_parts, axis=0) if n_sub > 1 else o_parts[0])

        o = o * lax.rsqrt(jnp.mean(o * o, axis=-1, keepdims=True) + GLA_NORM_EPS) * ng_ref[...]
        o = o * (gate * jax.nn.sigmoid(gate))
        o_ref[0, :, h * GLA_HEAD_V:(h + 1) * GLA_HEAD_V] = o.astype(o_ref.dtype)

        b_last = b[c - 1:c]
        k_bar = k * jnp.exp(b_last - b)
        decay_col = jnp.exp(_dot01_tn(glog[:, ks], ones_cv))
        s_ref[0, h] = s0 * decay_col + _dot_tn(k_bar, v)


def _gla(proj, s0, up_p, gk_b, norm_g, layer, chunk, zero_init):
    bsz, t, _ = proj.shape
    n_c = t // chunk
    s_spec = pl.BlockSpec((1, GLA_HEADS, GLA_HEAD_K, GLA_HEAD_V), lambda b, c: (b, 0, 0, 0))
    if zero_init:
        s0_arg = jnp.zeros((1, GLA_HEADS, GLA_HEAD_K, GLA_HEAD_V), F32)
        s0_spec = pl.BlockSpec((1, GLA_HEADS, GLA_HEAD_K, GLA_HEAD_V), lambda b, c: (0, 0, 0, 0))
    else:
        s0_arg = s0
        s0_spec = pl.BlockSpec((None, 1, GLA_HEADS, GLA_HEAD_K, GLA_HEAD_V),
                               lambda b, c: (layer, b, 0, 0, 0))
    return pl.pallas_call(
        functools.partial(_gla_kernel, zero_init=zero_init, sub=min(GLA_SUB, chunk)),
        grid=(bsz, n_c),
        in_specs=[
            pl.BlockSpec((1, chunk, SEG), lambda b, c: (b, c, 0)),
            s0_spec,
            pl.BlockSpec((None, LANES, GLA_KEY), lambda b, c: (layer, 0, 0)),
            pl.BlockSpec((None, 1, GLA_KEY), lambda b, c: (layer, 0, 0)),
            pl.BlockSpec((None, 1, GLA_HEAD_V), lambda b, c: (layer, 0, 0)),
        ],
        out_specs=[
            pl.BlockSpec((1, chunk, GLA_WIDTH), lambda b, c: (b, c, 0)),
            s_spec,
        ],
        out_shape=[
            jax.ShapeDtypeStruct((bsz, t, GLA_WIDTH), BF16),
            jax.ShapeDtypeStruct((bsz, GLA_HEADS, GLA_HEAD_K, GLA_HEAD_V), F32),
        ],
        compiler_params=pltpu.CompilerParams(
            dimension_semantics=("parallel", "arbitrary"), vmem_limit_bytes=VMEM_LIMIT_BYTES),
        name="gla",
    )(proj, s0_arg, up_p, gk_b, norm_g)


def _bdot(a, b):
    return lax.dot_general(a.astype(BF16), b.astype(BF16), (((2,), (1,)), ((0,), (0,))),
                           preferred_element_type=F32)


def _bdot_nt(a, b):
    return lax.dot_general(a.astype(BF16), b.astype(BF16), (((2,), (2,)), ((0,), (0,))),
                           preferred_element_type=F32)


def _bdot_tn(a, b):
    return lax.dot_general(a.astype(BF16), b.astype(BF16), (((1,), (1,)), ((0,), (0,))),
                           preferred_element_type=F32)


def _rwkv_kernel(pr_ref, sh0_ref, s0_ref, mu_ref, w0_ref, w2_ref, a0_ref, a2_ref, g2_ref, kk_ref,
                 ka_ref, rk_ref, lnw_ref, lnb_ref, o_ref, s_ref, sh_ref, *, zero_init):
    c_idx = pl.program_id(1)
    nb, c, _ = pr_ref.shape
    rows = nb * c
    log_c = c.bit_length() - 1
    hs = LANES // c
    lw = hs * RWKV_HEAD
    n_grp = RWKV_WIDTH // lw
    n_tile = lw // LANES
    g_all = nb * n_grp

    @pl.when(c_idx == 0)
    def _():
        sh_ref[...] = sh0_ref[...]
        if zero_init:
            s_ref[...] = jnp.zeros_like(s_ref)
        else:
            s_ref[...] = s0_ref[...]

    pr3 = pr_ref[...]
    pr = pr3.reshape(rows, SEG)
    prev_rows = jnp.broadcast_to(sh_ref[...], (nb, c, SEG)).reshape(rows, SEG)
    first_tok = (_iota((rows, 1), 0) & (c - 1)) == 0
    pr_prev = jnp.where(first_tok, prev_rows, pltpu.roll(pr, 1, 0))
    sh_ref[...] = pr3[:, c - 1:c, :]
    xr = pr + (pr_prev - pr) * mu_ref[...]

    r = xr[:, PR_R0:PR_R0 + RWKV_WIDTH]
    kr = xr[:, PR_K0:PR_K0 + RWKV_WIDTH]
    vr = xr[:, PR_V0:PR_V0 + RWKV_WIDTH]
    x_wa = xr[:, PR_WA0:PR_WA0 + LANES]
    x_g = xr[:, PR_G0:PR_G0 + PR_GW]
    w_inner = -_softplus(-(w0_ref[...] + _dot(jnp.tanh(x_wa), w2_ref[...]))) - 0.5
    log_w = -jnp.exp(w_inner)
    a = jax.nn.sigmoid(a0_ref[...] + _dot(x_wa, a2_ref[...]))
    gate = _dot(jax.nn.sigmoid(x_g), g2_ref[...])
    k_mod = kr * (1.0 + (a - 1.0) * ka_ref[...])
    kk_raw = kr * kk_ref[...]

    ones_blk = ((_iota((LANES, LANES), 0) < RWKV_HEAD) == (_iota((LANES, LANES), 1) < RWKV_HEAD)).astype(BF16)

    def head_sums(x):
        return jnp.concatenate([_segsum(x[:, j * LANES:(j + 1) * LANES], ones_blk)
                                for j in range(RWKV_WIDTH // LANES)], axis=1)

    kk = kk_raw / jnp.maximum(jnp.sqrt(head_sums(kk_raw * kk_raw)), 1e-12)
    beta = kk * a

    seq_r = lax.shift_right_logical(_iota((rows, rows), 0), log_c)
    seq_c = lax.shift_right_logical(_iota((rows, rows), 1), log_c)
    same_seq = seq_r == seq_c
    tril = jnp.where(jnp.logical_and(same_seq, _iota((rows, rows), 1) <= _iota((rows, rows), 0)), 1.0, 0.0)
    cum = _dot01(tril, log_w)
    cum_ex = cum - log_w
    cum_last = _dot01(jnp.where(same_seq, 1.0, 0.0), log_w)
    e_neg = jnp.exp(-cum)
    e_last = jnp.exp(cum_last - cum)

    head_mask = (lax.shift_right_logical(_iota((LANES, lw), 0), log_c)
                 == lax.shift_right_logical(_iota((LANES, lw), 1), RWKV_HEAD.bit_length() - 1))

    def groups(m):
        parts = []
        for s in range(nb):
            for p in range(n_grp):
                blk = m[s * c:(s + 1) * c, p * lw:(p + 1) * lw]
                parts.append(jnp.where(head_mask, jnp.concatenate([blk] * hs, axis=0), 0.0))
        return jnp.stack(parts, axis=0)

    al_s = groups(-kk * jnp.exp(cum_ex))
    rt_s = groups(r * jnp.exp(cum))
    bh_s = groups(beta * e_neg)
    kh_s = groups(k_mod * e_neg)
    v_s = groups(vr)
    bb_s = groups(beta * e_last)
    kb_s = groups(k_mod * e_last)

    row2 = _iota((LANES, LANES), 0)
    col2 = _iota((LANES, LANES), 1)
    same_head = lax.shift_right_logical(row2, log_c) == lax.shift_right_logical(col2, log_c)
    lower_strict = jnp.logical_and(same_head, col2 < row2)
    lower_incl = jnp.logical_and(same_head, col2 <= row2)
    eye2 = jnp.where(row2 == col2, 1.0, 0.0)

    a_b = jnp.where(lower_strict, _bdot_nt(al_s, bh_s), 0.0)
    a_k = jnp.where(lower_strict, _bdot_nt(al_s, kh_s), 0.0)
    b_b = jnp.where(lower_incl, _bdot_nt(rt_s, bh_s), 0.0)
    b_k = jnp.where(lower_incl, _bdot_nt(rt_s, kh_s), 0.0)
    t_inv = eye2 + a_b
    pw = a_b
    for _ in range(max(log_c - 1, 0)):
        pw = _bdot(pw, pw)
        t_inv = t_inv + _bdot(t_inv, pw)

    s0 = s_ref[...].reshape(g_all, n_tile, LANES, LANES)

    def with_state(x_s):
        return jnp.concatenate([_bdot_nt(x_s[:, :, j * LANES:(j + 1) * LANES], s0[:, j])
                                for j in range(n_tile)], axis=2)

    u_s = _bdot(t_inv, _bdot(a_k, v_s) + with_state(al_s))
    o_s = with_state(rt_s) + _bdot(b_b, u_s) + _bdot(b_k, v_s)

    new_tiles = []
    for j in range(n_tile):
        tl = slice(j * LANES, (j + 1) * LANES)
        decay = jnp.stack([jnp.exp(cum_last[s * c:s * c + 1, p * lw + j * LANES:p * lw + (j + 1) * LANES])
                           for s in range(nb) for p in range(n_grp)], axis=0)
        new_tiles.append(s0[:, j] * decay + _bdot_tn(u_s[:, :, tl], bb_s[:, :, tl])
                         + _bdot_tn(v_s[:, :, tl], kb_s[:, :, tl]))
    s_ref[...] = jnp.stack(new_tiles, axis=1).reshape(s_ref.shape)

    o_g = o_s[:, 0:c, :]
    for hh in range(1, hs):
        o_g = o_g + o_s[:, hh * c:(hh + 1) * c, :]
    o2 = jnp.concatenate([jnp.concatenate([o_g[s * n_grp + p] for p in range(n_grp)], axis=1)
                          for s in range(nb)], axis=0)

    mean = head_sums(o2) * (1.0 / RWKV_HEAD)
    cen = o2 - mean
    var = head_sums(cen * cen) * (1.0 / RWKV_HEAD)
    y = cen * lax.rsqrt(var + RWKV_GN_EPS) * lnw_ref[...] + lnb_ref[...]
    y = y + head_sums(r * k_mod * rk_ref[...]) * vr
    o_ref[...] = (y * gate).reshape(nb, c, RWKV_WIDTH).astype(o_ref.dtype)


def _rwkv(proj, shift0, s0_blk, wts, layer, chunk, zero_init):
    bsz, t, _ = proj.shape
    n_c = t // chunk
    nb = min(bsz, LANES // chunk, RWKV_SEQ_PER_BLOCK)
    assert bsz % nb == 0 and LANES % chunk == 0
    s_spec = pl.BlockSpec((nb, N_PAIR, LANES, LANES), lambda b, c: (b, 0, 0, 0))
    if zero_init:
        s0_arg = jnp.zeros((nb, N_PAIR, LANES, LANES), F32)
        s0_spec = pl.BlockSpec((nb, N_PAIR, LANES, LANES), lambda b, c: (0, 0, 0, 0))
    else:
        s0_arg = s0_blk
        s0_spec = pl.BlockSpec((None, nb, N_PAIR, LANES, LANES), lambda b, c: (layer, b, 0, 0, 0))
    vec = lambda n: pl.BlockSpec((None, 1, n), lambda b, c: (layer, 0, 0))
    mat = lambda m, n: pl.BlockSpec((None, m, n), lambda b, c: (layer, 0, 0))
    return pl.pallas_call(
        functools.partial(_rwkv_kernel, zero_init=zero_init),
        grid=(bsz // nb, n_c),
        in_specs=[
            pl.BlockSpec((nb, chunk, SEG), lambda b, c: (b, c, 1)),
            pl.BlockSpec((None, nb, 1, SEG), lambda b, c: (layer, b, 0, 0)),
            s0_spec,
            vec(SEG), vec(RWKV_WIDTH), mat(LANES, RWKV_WIDTH), vec(RWKV_WIDTH),
            mat(LANES, RWKV_WIDTH), mat(PR_GW, RWKV_WIDTH), vec(RWKV_WIDTH), vec(RWKV_WIDTH),
            vec(RWKV_WIDTH), vec(RWKV_WIDTH), vec(RWKV_WIDTH),
        ],
        out_specs=[
            pl.BlockSpec((nb, chunk, RWKV_WIDTH), lambda b, c: (b, c, 0)),
            s_spec,
            pl.BlockSpec((nb, 1, SEG), lambda b, c: (b, 0, 0)),
        ],
        out_shape=[
            jax.ShapeDtypeStruct((bsz, t, RWKV_WIDTH), BF16),
            jax.ShapeDtypeStruct((bsz, N_PAIR, LANES, LANES), F32),
            jax.ShapeDtypeStruct((bsz, 1, SEG), F32),
        ],
        compiler_params=pltpu.CompilerParams(
            dimension_semantics=("parallel", "arbitrary"), vmem_limit_bytes=VMEM_LIMIT_BYTES),
        name="rwkv",
    )(proj, shift0, s0_arg, *wts)


def _pad_rows(w, rows_before, rows_total):
    depth, r, n = w.shape
    return jnp.concatenate([jnp.zeros((depth, rows_before, n), w.dtype), w,
                            jnp.zeros((depth, rows_total - rows_before - r, n), w.dtype)], axis=1)


def _wkv_to_blockdiag(s):
    depth, bsz = s.shape[:2]
    s = s.reshape(depth, bsz, N_PAIR, 2, RWKV_HEAD, RWKV_HEAD)
    z = jnp.zeros_like(s[:, :, :, 0])
    top = jnp.concatenate([s[:, :, :, 0], z], axis=-1)
    bot = jnp.concatenate([z, s[:, :, :, 1]], axis=-1)
    return jnp.concatenate([top, bot], axis=-2)


def _wkv_from_blockdiag(s):
    bsz = s.shape[0]
    h0 = s[:, :, :RWKV_HEAD, :RWKV_HEAD]
    h1 = s[:, :, RWKV_HEAD:, RWKV_HEAD:]
    return jnp.stack([h0, h1], axis=2).reshape(bsz, RWKV_HEADS, RWKV_HEAD, RWKV_HEAD)


def _trunk(x, mods, state_gla, state_wkv_blk, state_shift_p, params, zero_init):
    (g_ffn1, w_ffn1_in, w_ffn1_out, g_mix, w_in_p, up_p, gk_b, norm_g, rwkv_wts, w_out, g_ffn2,
     w_ffn2_in, w_ffn2_out, g_final) = params
    depth = w_in_p.shape[0]
    bsz, t, _ = x.shape
    gla_chunk = min(GLA_CHUNK, t)
    rwkv_chunk = min(RWKV_CHUNK, t)
    new_gla, new_wkv, new_shift = [], [], []
    for layer in range(depth):
        x = _ffn(x, mods, layer, 0, g_ffn1, w_ffn1_in, w_ffn1_out, g_final, False)
        proj = _inproj(x, mods, layer, g_mix, w_in_p)
        o_gla, s_gla = _gla(proj, state_gla, up_p, gk_b, norm_g, layer, gla_chunk, zero_init)
        o_w, s_wkv, shift = _rwkv(proj, state_shift_p, state_wkv_blk, rwkv_wts, layer, rwkv_chunk,
                                  zero_init)
        x = _outproj(x, mods, layer, o_gla, o_w, w_out)
        x = _ffn(x, mods, layer, 6, g_ffn2, w_ffn2_in, w_ffn2_out, g_final, layer == depth - 1)
        new_gla.append(s_gla)
        new_wkv.append(_wkv_from_blockdiag(s_wkv))
        new_shift.append(shift[:, 0, :RWKV_IN])
    return x, jnp.stack(new_gla), jnp.stack(new_wkv), jnp.stack(new_shift)


def kernel(x_prompt, x_sample, c_prompt, c_sample, state_gla, state_wkv, state_shift, w_ada, b_ada, g_ffn1, w_ffn1_in, w_ffn1_out, g_mix, w_in, gla_gk_up, gla_gk_b, gla_norm_g, rwkv_mu, rwkv_w0, rwkv_w2, rwkv_a0, rwkv_a2, rwkv_g2, rwkv_k_k, rwkv_k_a, rwkv_r_k, rwkv_ln_w, rwkv_ln_b, w_out, g_ffn2, w_ffn2_in, w_ffn2_out, g_final):
    depth = w_in.shape[0]
    bp, bs = x_prompt.shape[0], x_sample.shape[0]
    assert x_prompt.shape[-1] == D_MODEL and w_in.shape[-1] == GLA_IN + RWKV_IN

    n_seq = bp + bs
    n_seq_p = -(-n_seq // SUBLANES) * SUBLANES
    c_all = jnp.concatenate([c_prompt, c_sample, jnp.zeros((n_seq_p - n_seq, D_MODEL), F32)], axis=0)
    mods = _adaln(c_all, w_ada, b_ada)
    mods_p = mods[:, :, :bp, None, :]
    mods_s = mods[:, :, bp:n_seq, :]

    zcol = lambda n: jnp.zeros((depth, D_MODEL, n), w_in.dtype)
    a0 = 2 * GLA_KEY + GLA_WIDTH
    w_in_p = jnp.concatenate([
        w_in[:, :, :a0], w_in[:, :, a0 + GLA_GATE_RANK:GLA_IN], w_in[:, :, a0:a0 + GLA_GATE_RANK],
        zcol(SEG - GLA_IN), w_in[:, :, GLA_IN:], zcol(SEG - RWKV_IN)], axis=-1)
    row = lambda v: v.reshape(depth, 1, -1)
    pad_seg = lambda v: jnp.pad(v, [(0, 0)] * (v.ndim - 1) + [(0, SEG - RWKV_IN)])
    up_p = _pad_rows(gla_gk_up, 0, LANES)
    rwkv_wts = (
        row(pad_seg(rwkv_mu)), row(rwkv_w0), _pad_rows(rwkv_w2, 0, LANES), row(rwkv_a0),
        _pad_rows(rwkv_a2, RWKV_W_LORA, LANES), _pad_rows(rwkv_g2, 0, PR_GW), row(rwkv_k_k),
        row(rwkv_k_a), row(rwkv_r_k), row(rwkv_ln_w), row(rwkv_ln_b))
    params = (row(g_ffn1), w_ffn1_in, w_ffn1_out, row(g_mix), w_in_p, up_p, row(gla_gk_b),
              row(gla_norm_g), rwkv_wts, w_out, row(g_ffn2), w_ffn2_in, w_ffn2_out,
              g_final.reshape(1, D_MODEL))

    shift_zero = jnp.zeros((depth, bp, 1, SEG), F32)
    y_p, gla_p, wkv_p, shift_p = _trunk(x_prompt, mods_p, None, None, shift_zero, params, True)
    shift_s0 = pad_seg(state_shift)[:, :, None, :]
    y_s, gla_s, wkv_s, shift_s = _trunk(x_sample, mods_s, state_gla, _wkv_to_blockdiag(state_wkv),
                                        shift_s0, params, False)
    return (y_p, y_s, gla_p, wkv_p, shift_p, gla_s, wkv_s, shift_s)
```

```python
import functools

import jax
import jax.numpy as jnp
from jax import lax
from jax.experimental import pallas as pl
from jax.experimental.pallas import tpu as pltpu

F32 = jnp.float32
BF16 = jnp.bfloat16

D_MODEL = 2048
D_FF = 5632
N_MOD = 9
NORM_EPS = 1e-6
GLA_HEADS = 4
GLA_HEAD_K = 128
GLA_HEAD_V = 256
GLA_KEY = GLA_HEADS * GLA_HEAD_K
GLA_WIDTH = GLA_HEADS * GLA_HEAD_V
GLA_GATE_RANK = 16
GLA_GATE_NORMALIZER = 16.0
GLA_NORM_EPS = 1e-5
GLA_IN = 2 * GLA_KEY + 2 * GLA_WIDTH + GLA_GATE_RANK
RWKV_HEAD = 64
RWKV_HEADS = 16
RWKV_WIDTH = RWKV_HEADS * RWKV_HEAD
RWKV_W_LORA = 64
RWKV_A_LORA = 64
RWKV_G_LORA = 160
RWKV_IN = 3 * RWKV_WIDTH + RWKV_W_LORA + RWKV_A_LORA + RWKV_G_LORA
RWKV_GN_EPS = 64e-5

LANES = 128
SUBLANES = 8
VMEM_LIMIT_BYTES = 60 * 1024 * 1024

SEG = 3456
PG_Q0, PG_K0, PG_V0, PG_G0, PG_A0 = 0, 512, 1024, 2048, 3072
PR_R0, PR_K0, PR_V0, PR_WA0, PR_G0 = 0, 1024, 2048, 3072, 3200
PR_GW = SEG - PR_G0
N_PAIR = RWKV_HEADS // 2

ROW_TILE = 1024
GLA_CHUNK = 64
GLA_SUB = 16
GLA_SEQ_PER_BLOCK = 8
RWKV_CHUNK = 64
RWKV_SEQ_PER_BLOCK = 4


def _dot(a, b):
    return jnp.dot(a.astype(BF16), b.astype(BF16), preferred_element_type=F32)


def _dot_nt(a, b):
    return lax.dot_general(a.astype(BF16), b.astype(BF16), (((1,), (1,)), ((), ())),
                           preferred_element_type=F32)


def _dot_tn(a, b):
    return lax.dot_general(a.astype(BF16), b.astype(BF16), (((0,), (0,)), ((), ())),
                           preferred_element_type=F32)


def _split3(x):
    hi = x.astype(BF16)
    r1 = x - hi.astype(F32)
    mid = r1.astype(BF16)
    lo = (r1 - mid.astype(F32)).astype(BF16)
    return hi, mid, lo


def _dot01(m01, x):
    m = m01.astype(BF16)
    hi, mid, lo = _split3(x)
    f = lambda t: jnp.dot(m, t, preferred_element_type=F32)
    return f(hi) + f(mid) + f(lo)


def _dot01_right(x, m01):
    m = m01.astype(BF16)
    hi, mid, lo = _split3(x)
    f = lambda t: jnp.dot(t, m, preferred_element_type=F32)
    return f(hi) + f(mid) + f(lo)


def _segsum(x, ones_blk):
    hi = x.astype(BF16)
    lo = (x - hi.astype(F32)).astype(BF16)
    return (jnp.dot(hi, ones_blk, preferred_element_type=F32)
            + jnp.dot(lo, ones_blk, preferred_element_type=F32))


def _iota(shape, dim):
    return lax.broadcasted_iota(jnp.int32, shape, dim)


def _softplus(z):
    return jnp.maximum(z, 0.0) + jnp.log(1.0 + jnp.exp(-jnp.abs(z)))


def _modulated_norm(x, g, shift, scale):
    y = x * lax.rsqrt(jnp.mean(x * x, axis=-1, keepdims=True) + NORM_EPS) * g
    return y * (1.0 + scale) + shift


def _adaln_kernel(c_ref, w_ref, b_ref, o_ref):
    c = c_ref[...]
    o_ref[...] = _dot(c * jax.nn.sigmoid(c), w_ref[...]) + b_ref[...]


def _adaln(c_all, w_ada, b_ada):
    depth = w_ada.shape[0]
    rows = c_all.shape[0]
    tn = 1024
    nn = D_MODEL // tn
    return pl.pallas_call(
        _adaln_kernel,
        grid=(depth, N_MOD, nn),
        in_specs=[
            pl.BlockSpec((rows, D_MODEL), lambda l, m, n: (0, 0)),
            pl.BlockSpec((None, D_MODEL, tn), lambda l, m, n: (l, 0, m * nn + n)),
            pl.BlockSpec((None, 1, tn), lambda l, m, n: (l, 0, m * nn + n)),
        ],
        out_specs=pl.BlockSpec((None, None, rows, tn), lambda l, m, n: (l, m, 0, n)),
        out_shape=jax.ShapeDtypeStruct((depth, N_MOD, rows, D_MODEL), F32),
        compiler_params=pltpu.CompilerParams(
            dimension_semantics=("parallel", "parallel", "parallel"),
            vmem_limit_bytes=VMEM_LIMIT_BYTES),
        name="adaln",
    )(c_all, w_ada, b_ada.reshape(depth, 1, N_MOD * D_MODEL))


def _row_tiling(bsz, t):
    tb = min(t, ROW_TILE)
    assert t % tb == 0 and tb % SUBLANES == 0
    sb = 1 if tb > SUBLANES else min(bsz, ROW_TILE // tb)
    assert bsz % sb == 0
    return sb, tb, t // tb


def _mod_spec(sb, n_t, layer, m, width=D_MODEL, col=lambda j: 0):
    if sb == 1:
        return pl.BlockSpec((None, None, 1, 1, width), lambda i, j: (layer, m, i // n_t, 0, col(j)))
    return pl.BlockSpec((None, None, sb, width), lambda i, j: (layer, m, i, col(j)))


SLAB = 256


def _slabs(sb, tb):
    n = min(SLAB, sb * tb)
    assert (sb * tb) % n == 0 and (sb == 1 or n % tb == 0)
    return [(r0, n) for r0 in range(0, sb * tb, n)]


def _tok_rows(x_ref, sb, tb, r0, n):
    if sb == 1:
        return x_ref[0, r0:r0 + n, :]
    return x_ref[r0 // tb:(r0 + n) // tb].reshape(n, x_ref.shape[-1])


def _put_rows(o_ref, sb, tb, r0, n, val):
    if sb == 1:
        o_ref[0, r0:r0 + n, :] = val
    else:
        o_ref[r0 // tb:(r0 + n) // tb] = val.reshape(n // tb, tb, val.shape[-1])


def _seq_rows(m_ref, sb, tb, r0, n):
    if sb == 1:
        return m_ref[0]
    n_seq = n // tb
    seq, tok = _iota((n, n_seq), 1), _iota((n, n_seq), 0)
    expand = jnp.where(jnp.logical_and(tok >= seq * tb, tok < (seq + 1) * tb), 1.0, 0.0)
    return _dot01(expand, m_ref[r0 // tb:(r0 + n) // tb])


def _write_modulated_norm(x_ref, g_ref, shift_ref, scale_ref, h_ref):
    sb, tb, _ = x_ref.shape
    for r0, n in _slabs(sb, tb):
        h = _modulated_norm(_tok_rows(x_ref, sb, tb, r0, n), g_ref[...],
                            _seq_rows(shift_ref, sb, tb, r0, n), _seq_rows(scale_ref, sb, tb, r0, n))
        h_ref[r0:r0 + n, :] = h.astype(BF16)


def _ffn_kernel(x_ref, shift_ref, scale_ref, gate_ref, g_ref, w1a_ref, w1b_ref, w2_ref, gfin_ref,
                o_ref, h_ref, *, final_norm):
    j = pl.program_id(1)
    sb, tb, d = x_ref.shape

    @pl.when(j == 0)
    def _():
        _write_modulated_norm(x_ref, g_ref, shift_ref, scale_ref, h_ref)
        o_ref[...] = jnp.zeros_like(o_ref)

    h = h_ref[...]
    u1 = jnp.dot(h, w1a_ref[...].astype(BF16), preferred_element_type=F32)
    u2 = jnp.dot(h, w1b_ref[...].astype(BF16), preferred_element_type=F32)
    act = (u1 * jax.nn.sigmoid(u1)) * u2
    o_ref[...] += _dot(act, w2_ref[...]).reshape(sb, tb, d)

    @pl.when(j == pl.num_programs(1) - 1)
    def _():
        for r0, n in _slabs(sb, tb):
            y = (_tok_rows(x_ref, sb, tb, r0, n)
                 + 0.5 * _seq_rows(gate_ref, sb, tb, r0, n) * _tok_rows(o_ref, sb, tb, r0, n))
            if final_norm:
                y = y * lax.rsqrt(jnp.mean(y * y, axis=-1, keepdims=True) + NORM_EPS) * gfin_ref[...]
            _put_rows(o_ref, sb, tb, r0, n, y)


def _ffn(x, mods, layer, m0, g, w_in, w_out, g_final, final_norm):
    bsz, t, d = x.shape
    sb, tb, n_t = _row_tiling(bsz, t)
    tf = 512
    nf = D_FF // tf
    x_spec = pl.BlockSpec((sb, tb, d), lambda i, j: (i // n_t, i % n_t, 0), pipeline_mode=pl.Buffered(1))
    return pl.pallas_call(
        functools.partial(_ffn_kernel, final_norm=final_norm),
        grid=((bsz // sb) * n_t, nf),
        in_specs=[
            x_spec,
            _mod_spec(sb, n_t, layer, m0), _mod_spec(sb, n_t, layer, m0 + 1),
            _mod_spec(sb, n_t, layer, m0 + 2),
            pl.BlockSpec((None, 1, d), lambda i, j: (layer, 0, 0)),
            pl.BlockSpec((None, d, tf), lambda i, j: (layer, 0, j)),
            pl.BlockSpec((None, d, tf), lambda i, j: (layer, 0, j + nf)),
            pl.BlockSpec((None, tf, d), lambda i, j: (layer, j, 0)),
            pl.BlockSpec((1, d), lambda i, j: (0, 0)),
        ],
        out_specs=x_spec,
        out_shape=jax.ShapeDtypeStruct(x.shape, F32),
        scratch_shapes=[pltpu.VMEM((sb * tb, d), BF16)],
        compiler_params=pltpu.CompilerParams(
            dimension_semantics=("parallel", "arbitrary"), vmem_limit_bytes=VMEM_LIMIT_BYTES),
        name="ffn",
    )(x, mods, mods, mods, g, w_in, w_in, w_out, g_final)


def _inproj_kernel(x_ref, shift_ref, scale_ref, g_ref, w_ref, o_ref, h_ref):
    sb, tb, d = x_ref.shape

    @pl.when(pl.program_id(1) == 0)
    def _():
        _write_modulated_norm(x_ref, g_ref, shift_ref, scale_ref, h_ref)

    o_ref[...] = jnp.dot(h_ref[...], w_ref[...].astype(BF16),
                         preferred_element_type=F32).reshape(o_ref.shape)


def _inproj(x, mods, layer, g, w_in_p):
    bsz, t, d = x.shape
    sb, tb, n_t = _row_tiling(bsz, t)
    n_out = w_in_p.shape[-1]
    tn = 1152
    assert n_out % tn == 0
    return pl.pallas_call(
        _inproj_kernel,
        grid=((bsz // sb) * n_t, n_out // tn),
        in_specs=[
            pl.BlockSpec((sb, tb, d), lambda i, j: (i // n_t, i % n_t, 0),
                         pipeline_mode=pl.Buffered(1)),
            _mod_spec(sb, n_t, layer, 3), _mod_spec(sb, n_t, layer, 4),
            pl.BlockSpec((None, 1, d), lambda i, j: (layer, 0, 0)),
            pl.BlockSpec((None, d, tn), lambda i, j: (layer, 0, j)),
        ],
        out_specs=pl.BlockSpec((sb, tb, tn), lambda i, j: (i // n_t, i % n_t, j)),
        out_shape=jax.ShapeDtypeStruct((bsz, t, n_out), F32),
        scratch_shapes=[pltpu.VMEM((sb * tb, d), BF16)],
        compiler_params=pltpu.CompilerParams(
            dimension_semantics=("parallel", "arbitrary"), vmem_limit_bytes=VMEM_LIMIT_BYTES),
        name="inproj",
    )(x, mods, mods, g, w_in_p)


def _outproj_kernel(x_ref, gate_ref, og_ref, ow_ref, wg_ref, ww_ref, o_ref):
    sb, tb, tn = x_ref.shape
    og = og_ref[...].reshape(sb * tb, GLA_WIDTH)
    ow = ow_ref[...].reshape(sb * tb, RWKV_WIDTH)
    y = (jnp.dot(og, wg_ref[...].astype(BF16), preferred_element_type=F32)
         + jnp.dot(ow, ww_ref[...].astype(BF16), preferred_element_type=F32))
    o_ref[...] = (x_ref[...].reshape(sb * tb, tn)
                  + _seq_rows(gate_ref, sb, tb, 0, sb * tb) * y).reshape(sb, tb, tn)


def _outproj(x, mods, layer, o_gla, o_w, w_out):
    bsz, t, d = x.shape
    sb, tb, n_t = _row_tiling(bsz, t)
    tn = 1024
    x_spec = pl.BlockSpec((sb, tb, tn), lambda i, j: (i // n_t, i % n_t, j))
    o_spec = pl.BlockSpec((sb, tb, GLA_WIDTH), lambda i, j: (i // n_t, i % n_t, 0))
    return pl.pallas_call(
        _outproj_kernel,
        grid=((bsz // sb) * n_t, d // tn),
        in_specs=[
            x_spec,
            _mod_spec(sb, n_t, layer, 5, width=tn, col=lambda j: j),
            o_spec, o_spec,
            pl.BlockSpec((None, GLA_WIDTH, tn), lambda i, j: (layer, 0, j)),
            pl.BlockSpec((None, RWKV_WIDTH, tn), lambda i, j: (layer, 1, j)),
        ],
        out_specs=x_spec,
        out_shape=jax.ShapeDtypeStruct(x.shape, F32),
        compiler_params=pltpu.CompilerParams(
            dimension_semantics=("parallel", "arbitrary"), vmem_limit_bytes=VMEM_LIMIT_BYTES),
        name="outproj",
    )(x, mods, o_gla, o_w, w_out, w_out)


def _gla_kernel(pg_ref, s0_ref, up_ref, gb_ref, ng_ref, o_ref, s_ref, *, zero_init, sub):
    c_idx = pl.program_id(1)
    nb, c, _ = pg_ref.shape
    rows = nb * c
    log_c = c.bit_length() - 1
    n_sub = c // sub

    @pl.when(c_idx == 0)
    def _():
        if zero_init:
            s_ref[...] = jnp.zeros_like(s_ref)
        else:
            s_ref[...] = s0_ref[...]

    x_a = pg_ref[:, :, PG_A0:PG_A0 + LANES].reshape(rows, LANES)
    z = _dot(x_a, up_ref[...]) + gb_ref[...]
    glog = -_softplus(-z) / GLA_GATE_NORMALIZER
    r_i, c_i = _iota((rows, rows), 0), _iota((rows, rows), 1)
    same_seq = lax.shift_right_logical(r_i, log_c) == lax.shift_right_logical(c_i, log_c)
    tril = jnp.where(jnp.logical_and(same_seq, c_i <= r_i), 1.0, 0.0)
    b_all = _dot01(tril, glog)
    causal_ss = _iota((sub, sub), 1) <= _iota((sub, sub), 0)
    lane_ss = _iota((sub, sub), 1)

    for s in range(nb):
        for h in range(GLA_HEADS):
            q = pg_ref[s, :, PG_Q0 + h * GLA_HEAD_K:PG_Q0 + (h + 1) * GLA_HEAD_K] * (GLA_HEAD_K ** -0.5)
            k = pg_ref[s, :, PG_K0 + h * GLA_HEAD_K:PG_K0 + (h + 1) * GLA_HEAD_K]
            v = pg_ref[s, :, PG_V0 + h * GLA_HEAD_V:PG_V0 + (h + 1) * GLA_HEAD_V]
            gate = pg_ref[s, :, PG_G0 + h * GLA_HEAD_V:PG_G0 + (h + 1) * GLA_HEAD_V]
            b = b_all[s * c:(s + 1) * c, h * GLA_HEAD_K:(h + 1) * GLA_HEAD_K]
            s0 = s_ref[s, h]

            o_inter = _dot(q * jnp.exp(b), s0)
            o_parts = []
            for i_sub in range(n_sub):
                r0 = i_sub * sub
                rs = slice(r0, r0 + sub)
                q_i, b_i = q[rs], b[rs]
                diag = jnp.zeros((sub, sub), F32)
                for jj in range(sub):
                    dec = jnp.exp(jnp.minimum(b_i - b[r0 + jj:r0 + jj + 1], 0.0))
                    col = jnp.sum(q_i * dec * k[r0 + jj:r0 + jj + 1], axis=-1, keepdims=True)
                    diag = jnp.where(lane_ss == jj, col, diag)
                o_i = _dot(jnp.where(causal_ss, diag, 0.0), v[rs])
                if i_sub > 0:
                    b_start = b[r0 - 1:r0]
                    q_rel = q_i * jnp.exp(b_i - b_start)
                    k_rel = k[:r0] * jnp.exp(b_start - b[:r0])
                    o_i = o_i + _dot(_dot_nt(q_rel, k_rel), v[:r0])
                o_parts.append(o_i)
            o = o_inter + (jnp.concatenate(o_parts, axis=0) if n_sub > 1 else o_parts[0])

            o = o * lax.rsqrt(jnp.mean(o * o, axis=-1, keepdims=True) + GLA_NORM_EPS) * ng_ref[...]
            o = o * (gate * jax.nn.sigmoid(gate))
            o_ref[s, :, h * GLA_HEAD_V:(h + 1) * GLA_HEAD_V] = o.astype(o_ref.dtype)

            b_last = b[c - 1:c]
            k_bar = k * jnp.exp(b_last - b)
            col = jnp.exp(jnp.broadcast_to(b_last, (GLA_HEAD_K, GLA_HEAD_K)).T)
            decay_col = jnp.concatenate([col] * (GLA_HEAD_V // GLA_HEAD_K), axis=1)
            s_ref[s, h] = s0 * decay_col + _dot_tn(k_bar, v)


def _gla(proj, s0, up_p, gk_b, norm_g, layer, chunk, zero_init):
    bsz, t, _ = proj.shape
    n_c = t // chunk
    nb = min(bsz, LANES // chunk, GLA_SEQ_PER_BLOCK)
    assert bsz % nb == 0
    s_spec = pl.BlockSpec((nb, GLA_HEADS, GLA_HEAD_K, GLA_HEAD_V), lambda b, c: (b, 0, 0, 0))
    if zero_init:
        s0_arg = jnp.zeros((nb, GLA_HEADS, GLA_HEAD_K, GLA_HEAD_V), F32)
        s0_spec = pl.BlockSpec((nb, GLA_HEADS, GLA_HEAD_K, GLA_HEAD_V), lambda b, c: (0, 0, 0, 0))
    else:
        s0_arg = s0
        s0_spec = pl.BlockSpec((None, nb, GLA_HEADS, GLA_HEAD_K, GLA_HEAD_V),
                               lambda b, c: (layer, b, 0, 0, 0))
    return pl.pallas_call(
        functools.partial(_gla_kernel, zero_init=zero_init, sub=min(GLA_SUB, chunk)),
        grid=(bsz // nb, n_c),
        in_specs=[
            pl.BlockSpec((nb, chunk, SEG), lambda b, c: (b, c, 0)),
            s0_spec,
            pl.BlockSpec((None, LANES, GLA_KEY), lambda b, c: (layer, 0, 0)),
            pl.BlockSpec((None, 1, GLA_KEY), lambda b, c: (layer, 0, 0)),
            pl.BlockSpec((None, 1, GLA_HEAD_V), lambda b, c: (layer, 0, 0)),
        ],
        out_specs=[
            pl.BlockSpec((nb, chunk, GLA_WIDTH), lambda b, c: (b, c, 0)),
            s_spec,
        ],
        out_shape=[
            jax.ShapeDtypeStruct((bsz, t, GLA_WIDTH), BF16),
            jax.ShapeDtypeStruct((bsz, GLA_HEADS, GLA_HEAD_K, GLA_HEAD_V), F32),
        ],
        compiler_params=pltpu.CompilerParams(
            dimension_semantics=("parallel", "arbitrary"), vmem_limit_bytes=VMEM_LIMIT_BYTES),
        name="gla",
    )(proj, s0_arg, up_p, gk_b, norm_g)


def _bdot(a, b):
    return lax.dot_general(a.astype(BF16), b.astype(BF16), (((2,), (1,)), ((0,), (0,))),
                           preferred_element_type=F32)


def _bdot_nt(a, b):
    return lax.dot_general(a.astype(BF16), b.astype(BF16), (((2,), (2,)), ((0,), (0,))),
                           preferred_element_type=F32)


def _bdot_tn(a, b):
    return lax.dot_general(a.astype(BF16), b.astype(BF16), (((1,), (1,)), ((0,), (0,))),
                           preferred_element_type=F32)


def _pair_tiles(s_heads):
    n = s_heads.shape[0]
    dup = jnp.where(_iota((RWKV_HEAD, LANES), 0) == (_iota((RWKV_HEAD, LANES), 1) & (RWKV_HEAD - 1)), 1.0, 0.0)
    wide = _dot01_right(s_heads.reshape(n * LANES, RWKV_HEAD), dup).reshape(n, LANES, LANES)
    return jnp.where(_pair_mask(), wide, 0.0)


def _pair_heads(tiles):
    n = tiles.shape[0]
    fold = jnp.where((_iota((LANES, RWKV_HEAD), 0) & (RWKV_HEAD - 1)) == _iota((LANES, RWKV_HEAD), 1), 1.0, 0.0)
    diag = jnp.where(_pair_mask(), tiles, 0.0).reshape(n * LANES, LANES)
    return _dot01_right(diag, fold).reshape(n, LANES, RWKV_HEAD)


def _pair_mask():
    return (_iota((LANES, LANES), 0) < RWKV_HEAD) == (_iota((LANES, LANES), 1) < RWKV_HEAD)


def _rwkv_kernel(pr_ref, sh0_ref, s0_ref, mu_ref, w0_ref, w2_ref, a0_ref, a2_ref, g2_ref, kk_ref,
                 ka_ref, rk_ref, lnw_ref, lnb_ref, o_ref, s_out_ref, sh_ref, s_ref, *, zero_init):
    c_idx = pl.program_id(1)
    nb, c, _ = pr_ref.shape
    rows = nb * c
    log_c = c.bit_length() - 1
    hs = LANES // c
    lw = hs * RWKV_HEAD
    n_grp = RWKV_WIDTH // lw
    n_tile = lw // LANES
    g_all = nb * n_grp

    @pl.when(c_idx == 0)
    def _():
        sh_ref[...] = sh0_ref[...]
        if zero_init:
            s_ref[...] = jnp.zeros_like(s_ref)
        else:
            s_ref[...] = _pair_tiles(s0_ref[...].reshape(nb * N_PAIR, LANES, RWKV_HEAD)).reshape(s_ref.shape)

    pr3 = pr_ref[...]
    pr = pr3.reshape(rows, SEG)
    prev_rows = jnp.broadcast_to(sh_ref[...], (nb, c, SEG)).reshape(rows, SEG)
    first_tok = (_iota((rows, 1), 0) & (c - 1)) == 0
    pr_prev = jnp.where(first_tok, prev_rows, pltpu.roll(pr, 1, 0))
    sh_ref[...] = pr3[:, c - 1:c, :]
    xr = pr + (pr_prev - pr) * mu_ref[...]

    r = xr[:, PR_R0:PR_R0 + RWKV_WIDTH]
    kr = xr[:, PR_K0:PR_K0 + RWKV_WIDTH]
    vr = xr[:, PR_V0:PR_V0 + RWKV_WIDTH]
    x_wa = xr[:, PR_WA0:PR_WA0 + LANES]
    x_g = xr[:, PR_G0:PR_G0 + PR_GW]
    w_inner = -_softplus(-(w0_ref[...] + _dot(jnp.tanh(x_wa), w2_ref[...]))) - 0.5
    log_w = -jnp.exp(w_inner)
    a = jax.nn.sigmoid(a0_ref[...] + _dot(x_wa, a2_ref[...]))
    gate = _dot(jax.nn.sigmoid(x_g), g2_ref[...])
    k_mod = kr * (1.0 + (a - 1.0) * ka_ref[...])
    kk_raw = kr * kk_ref[...]

    ones_blk = ((_iota((LANES, LANES), 0) < RWKV_HEAD) == (_iota((LANES, LANES), 1) < RWKV_HEAD)).astype(BF16)

    def head_sums(x):
        return jnp.concatenate([_segsum(x[:, j * LANES:(j + 1) * LANES], ones_blk)
                                for j in range(RWKV_WIDTH // LANES)], axis=1)

    kk = kk_raw / jnp.maximum(jnp.sqrt(head_sums(kk_raw * kk_raw)), 1e-12)
    beta = kk * a

    seq_r = lax.shift_right_logical(_iota((rows, rows), 0), log_c)
    seq_c = lax.shift_right_logical(_iota((rows, rows), 1), log_c)
    same_seq = seq_r == seq_c
    tril = jnp.where(jnp.logical_and(same_seq, _iota((rows, rows), 1) <= _iota((rows, rows), 0)), 1.0, 0.0)
    cum = _dot01(tril, log_w)
    cum_ex = cum - log_w
    cum_last = jnp.concatenate(
        [jnp.broadcast_to(cum[(s + 1) * c - 1:(s + 1) * c], (c, RWKV_WIDTH)) for s in range(nb)], axis=0)
    e_neg = jnp.exp(-cum)
    e_last = jnp.exp(cum_last - cum)

    head_mask = (lax.shift_right_logical(_iota((LANES, lw), 0), log_c)
                 == lax.shift_right_logical(_iota((LANES, lw), 1), RWKV_HEAD.bit_length() - 1))

    def groups(m):
        parts = []
        for s in range(nb):
            for p in range(n_grp):
                blk = m[s * c:(s + 1) * c, p * lw:(p + 1) * lw]
                parts.append(jnp.where(head_mask, jnp.concatenate([blk] * hs, axis=0), 0.0))
        return jnp.stack(parts, axis=0)

    al_s = groups(-kk * jnp.exp(cum_ex))
    rt_s = groups(r * jnp.exp(cum))
    bh_s = groups(beta * e_neg)
    kh_s = groups(k_mod * e_neg)
    v_s = groups(vr)
    bb_s = groups(beta * e_last)
    kb_s = groups(k_mod * e_last)

    row2 = _iota((LANES, LANES), 0)
    col2 = _iota((LANES, LANES), 1)
    same_head = lax.shift_right_logical(row2, log_c) == lax.shift_right_logical(col2, log_c)
    lower_strict = jnp.logical_and(same_head, col2 < row2)
    lower_incl = jnp.logical_and(same_head, col2 <= row2)
    eye2 = jnp.where(row2 == col2, 1.0, 0.0)

    quad = _bdot_nt(jnp.concatenate([al_s, rt_s], axis=1), jnp.concatenate([bh_s, kh_s], axis=1))
    a_b = jnp.where(lower_strict, quad[:, :LANES, :LANES], 0.0)
    a_k = jnp.where(lower_strict, quad[:, :LANES, LANES:], 0.0)
    b_bk = jnp.where(jnp.concatenate([lower_incl, lower_incl], axis=1), quad[:, LANES:, :], 0.0)
    t_inv = eye2 + a_b
    pw = a_b
    for _ in range(max(log_c - 1, 0)):
        pw = _bdot(pw, pw)
        t_inv = t_inv + _bdot(t_inv, pw)

    s0 = s_ref[...].reshape(g_all, n_tile, LANES, LANES)

    def with_state(x_s):
        return jnp.concatenate([_bdot_nt(x_s[:, :, j * LANES:(j + 1) * LANES], s0[:, j])
                                for j in range(n_tile)], axis=2)

    u_s = _bdot(t_inv, _bdot(a_k, v_s) + with_state(al_s))
    o_s = with_state(rt_s) + _bdot(b_bk, jnp.concatenate([u_s, v_s], axis=1))

    new_tiles = []
    for j in range(n_tile):
        tl = slice(j * LANES, (j + 1) * LANES)
        decay = jnp.stack([jnp.exp(cum_last[s * c:s * c + 1, p * lw + j * LANES:p * lw + (j + 1) * LANES])
                           for s in range(nb) for p in range(n_grp)], axis=0)
        new_tiles.append(s0[:, j] * decay
                         + _bdot_tn(jnp.concatenate([u_s[:, :, tl], v_s[:, :, tl]], axis=1),
                                    jnp.concatenate([bb_s[:, :, tl], kb_s[:, :, tl]], axis=1)))
    s_new = jnp.stack(new_tiles, axis=1)
    s_ref[...] = s_new.reshape(s_ref.shape)

    @pl.when(c_idx == pl.num_programs(1) - 1)
    def _():
        s_out_ref[...] = _pair_heads(s_new.reshape(nb * N_PAIR, LANES, LANES)).reshape(s_out_ref.shape)

    o_g = o_s[:, 0:c, :]
    for hh in range(1, hs):
        o_g = o_g + o_s[:, hh * c:(hh + 1) * c, :]
    o2 = jnp.concatenate([jnp.concatenate([o_g[s * n_grp + p] for p in range(n_grp)], axis=1)
                          for s in range(nb)], axis=0)

    mean = head_sums(o2) * (1.0 / RWKV_HEAD)
    cen = o2 - mean
    var = head_sums(cen * cen) * (1.0 / RWKV_HEAD)
    y = cen * lax.rsqrt(var + RWKV_GN_EPS) * lnw_ref[...] + lnb_ref[...]
    y = y + head_sums(r * k_mod * rk_ref[...]) * vr
    o_ref[...] = (y * gate).reshape(nb, c, RWKV_WIDTH).astype(o_ref.dtype)


def _rwkv(proj, shift0, s0_blk, wts, layer, chunk, zero_init):
    bsz, t, _ = proj.shape
    n_c = t // chunk
    nb = min(bsz, LANES // chunk, RWKV_SEQ_PER_BLOCK)
    assert bsz % nb == 0 and LANES % chunk == 0
    s_spec = pl.BlockSpec((nb, N_PAIR, LANES, RWKV_HEAD), lambda b, c: (b, 0, 0, 0))
    if zero_init:
        s0_arg = jnp.zeros((nb, N_PAIR, LANES, RWKV_HEAD), F32)
        s0_spec = pl.BlockSpec((nb, N_PAIR, LANES, RWKV_HEAD), lambda b, c: (0, 0, 0, 0))
    else:
        s0_arg = s0_blk
        s0_spec = pl.BlockSpec((None, nb, N_PAIR, LANES, RWKV_HEAD), lambda b, c: (layer, b, 0, 0, 0))
    vec = lambda n: pl.BlockSpec((None, 1, n), lambda b, c: (layer, 0, 0))
    mat = lambda m, n: pl.BlockSpec((None, m, n), lambda b, c: (layer, 0, 0))
    return pl.pallas_call(
        functools.partial(_rwkv_kernel, zero_init=zero_init),
        grid=(bsz // nb, n_c),
        in_specs=[
            pl.BlockSpec((nb, chunk, SEG), lambda b, c: (b, c, 1)),
            pl.BlockSpec((None, nb, 1, SEG), lambda b, c: (layer, b, 0, 0)),
            s0_spec,
            vec(SEG), vec(RWKV_WIDTH), mat(LANES, RWKV_WIDTH), vec(RWKV_WIDTH),
            mat(LANES, RWKV_WIDTH), mat(PR_GW, RWKV_WIDTH), vec(RWKV_WIDTH), vec(RWKV_WIDTH),
            vec(RWKV_WIDTH), vec(RWKV_WIDTH), vec(RWKV_WIDTH),
        ],
        out_specs=[
            pl.BlockSpec((nb, chunk, RWKV_WIDTH), lambda b, c: (b, c, 0)),
            s_spec,
            pl.BlockSpec((nb, 1, SEG), lambda b, c: (b, 0, 0)),
        ],
        out_shape=[
            jax.ShapeDtypeStruct((bsz, t, RWKV_WIDTH), BF16),
            jax.ShapeDtypeStruct((bsz, N_PAIR, LANES, RWKV_HEAD), F32),
            jax.ShapeDtypeStruct((bsz, 1, SEG), F32),
        ],
        scratch_shapes=[pltpu.VMEM((nb, N_PAIR, LANES, LANES), F32)],
        compiler_params=pltpu.CompilerParams(
            dimension_semantics=("parallel", "arbitrary"), vmem_limit_bytes=VMEM_LIMIT_BYTES),
        name="rwkv",
    )(proj, shift0, s0_arg, *wts)


def _pad_rows(w, rows_before, rows_total):
    depth, r, n = w.shape
    return jnp.concatenate([jnp.zeros((depth, rows_before, n), w.dtype), w,
                            jnp.zeros((depth, rows_total - rows_before - r, n), w.dtype)], axis=1)


def _trunk(x, mods, state_gla, state_wkv_blk, state_shift_p, params, zero_init):
    (g_ffn1, w_ffn1_in, w_ffn1_out, g_mix, w_in_p, up_p, gk_b, norm_g, rwkv_wts, w_out, g_ffn2,
     w_ffn2_in, w_ffn2_out, g_final) = params
    depth = w_in_p.shape[0]
    bsz, t, _ = x.shape
    gla_chunk = min(GLA_CHUNK, t)
    rwkv_chunk = min(RWKV_CHUNK, t)
    new_gla, new_wkv, new_shift = [], [], []
    for layer in range(depth):
        x = _ffn(x, mods, layer, 0, g_ffn1, w_ffn1_in, w_ffn1_out, g_final, False)
        proj = _inproj(x, mods, layer, g_mix, w_in_p)
        o_gla, s_gla = _gla(proj, state_gla, up_p, gk_b, norm_g, layer, gla_chunk, zero_init)
        o_w, s_wkv, shift = _rwkv(proj, state_shift_p, state_wkv_blk, rwkv_wts, layer, rwkv_chunk,
                                  zero_init)
        x = _outproj(x, mods, layer, o_gla, o_w, w_out)
        x = _ffn(x, mods, layer, 6, g_ffn2, w_ffn2_in, w_ffn2_out, g_final, layer == depth - 1)
        new_gla.append(s_gla)
        new_wkv.append(s_wkv.reshape(bsz, RWKV_HEADS, RWKV_HEAD, RWKV_HEAD))
        new_shift.append(shift[:, 0, :RWKV_IN])
    return x, jnp.stack(new_gla), jnp.stack(new_wkv), jnp.stack(new_shift)


def kernel(x_prompt, x_sample, c_prompt, c_sample, state_gla, state_wkv, state_shift, w_ada, b_ada, g_ffn1, w_ffn1_in, w_ffn1_out, g_mix, w_in, gla_gk_up, gla_gk_b, gla_norm_g, rwkv_mu, rwkv_w0, rwkv_w2, rwkv_a0, rwkv_a2, rwkv_g2, rwkv_k_k, rwkv_k_a, rwkv_r_k, rwkv_ln_w, rwkv_ln_b, w_out, g_ffn2, w_ffn2_in, w_ffn2_out, g_final):
    depth = w_in.shape[0]
    bp, bs = x_prompt.shape[0], x_sample.shape[0]
    assert x_prompt.shape[-1] == D_MODEL and w_in.shape[-1] == GLA_IN + RWKV_IN

    n_seq = bp + bs
    n_seq_p = -(-n_seq // SUBLANES) * SUBLANES
    c_all = jnp.concatenate([c_sample, c_prompt, jnp.zeros((n_seq_p - n_seq, D_MODEL), F32)], axis=0)
    mods = _adaln(c_all, w_ada, b_ada)
    mods_p = mods[:, :, bs:n_seq, None, :]

    zcol = lambda n: jnp.zeros((depth, D_MODEL, n), w_in.dtype)
    a0 = 2 * GLA_KEY + GLA_WIDTH
    w_in_p = jnp.concatenate([
        w_in[:, :, :a0], w_in[:, :, a0 + GLA_GATE_RANK:GLA_IN], w_in[:, :, a0:a0 + GLA_GATE_RANK],
        zcol(SEG - GLA_IN), w_in[:, :, GLA_IN:], zcol(SEG - RWKV_IN)], axis=-1)
    row = lambda v: v.reshape(depth, 1, -1)
    pad_seg = lambda v: jnp.pad(v, [(0, 0)] * (v.ndim - 1) + [(0, SEG - RWKV_IN)])
    up_p = _pad_rows(gla_gk_up, 0, LANES)
    rwkv_wts = (
        row(pad_seg(rwkv_mu)), row(rwkv_w0), _pad_rows(rwkv_w2, 0, LANES), row(rwkv_a0),
        _pad_rows(rwkv_a2, RWKV_W_LORA, LANES), _pad_rows(rwkv_g2, 0, PR_GW), row(rwkv_k_k),
        row(rwkv_k_a), row(rwkv_r_k), row(rwkv_ln_w), row(rwkv_ln_b))
    params = (row(g_ffn1), w_ffn1_in, w_ffn1_out, row(g_mix), w_in_p, up_p, row(gla_gk_b),
              row(gla_norm_g), rwkv_wts, w_out, row(g_ffn2), w_ffn2_in, w_ffn2_out,
              g_final.reshape(1, D_MODEL))

    shift_zero = jnp.zeros((depth, bp, 1, SEG), F32)
    y_p, gla_p, wkv_p, shift_p = _trunk(x_prompt, mods_p, None, None, shift_zero, params, True)
    shift_s0 = pad_seg(state_shift)[:, :, None, :]
    wkv_s0 = state_wkv.reshape(depth, bs, N_PAIR, LANES, RWKV_HEAD)
    y_s, gla_s, wkv_s, shift_s = _trunk(x_sample, mods, state_gla, wkv_s0, shift_s0, params, False)
    return (y_p, y_s, gla_p, wkv_p, shift_p, gla_s, wkv_s, shift_s)
```

```python
import functools

import jax
import jax.numpy as jnp
from jax import lax
from jax.experimental import pallas as pl
from jax.experimental.pallas import tpu as pltpu

F32 = jnp.float32
BF16 = jnp.bfloat16

D_MODEL = 2048
D_FF = 5632
N_MOD = 9
NORM_EPS = 1e-6
GLA_HEADS = 4
GLA_HEAD_K = 128
GLA_HEAD_V = 256
GLA_KEY = GLA_HEADS * GLA_HEAD_K
GLA_WIDTH = GLA_HEADS * GLA_HEAD_V
GLA_GATE_RANK = 16
GLA_GATE_NORMALIZER = 16.0
GLA_NORM_EPS = 1e-5
GLA_IN = 2 * GLA_KEY + 2 * GLA_WIDTH + GLA_GATE_RANK
RWKV_HEAD = 64
RWKV_HEADS = 16
RWKV_WIDTH = RWKV_HEADS * RWKV_HEAD
RWKV_W_LORA = 64
RWKV_A_LORA = 64
RWKV_G_LORA = 160
RWKV_IN = 3 * RWKV_WIDTH + RWKV_W_LORA + RWKV_A_LORA + RWKV_G_LORA
RWKV_GN_EPS = 64e-5

LANES = 128
SUBLANES = 8
VMEM_LIMIT_BYTES = 60 * 1024 * 1024

SEG = 3456
PG_Q0, PG_K0, PG_V0, PG_G0, PG_A0 = 0, 512, 1024, 2048, 3072
PR_R0, PR_K0, PR_V0, PR_WA0, PR_G0 = 0, 1024, 2048, 3072, 3200
PR_GW = SEG - PR_G0
N_PAIR = RWKV_HEADS // 2

ROW_TILE = 1024
INPROJ_ROW_TILE = 2048
GLA_CHUNK = 64
GLA_SUB = 16
GLA_SEQ_PER_BLOCK = 8
RWKV_CHUNK = 64
RWKV_SEQ_PER_BLOCK = 8


def _dot(a, b):
    return jnp.dot(a.astype(BF16), b.astype(BF16), preferred_element_type=F32)


def _dot_nt(a, b):
    return lax.dot_general(a.astype(BF16), b.astype(BF16), (((1,), (1,)), ((), ())),
                           preferred_element_type=F32)


def _dot_tn(a, b):
    return lax.dot_general(a.astype(BF16), b.astype(BF16), (((0,), (0,)), ((), ())),
                           preferred_element_type=F32)


def _split3(x):
    hi = x.astype(BF16)
    r1 = x - hi.astype(F32)
    mid = r1.astype(BF16)
    lo = (r1 - mid.astype(F32)).astype(BF16)
    return hi, mid, lo


def _dot01(m01, x):
    m = m01.astype(BF16)
    hi, mid, lo = _split3(x)
    f = lambda t: jnp.dot(m, t, preferred_element_type=F32)
    return f(hi) + f(mid) + f(lo)


def _segsum(x, ones_blk):
    hi = x.astype(BF16)
    lo = (x - hi.astype(F32)).astype(BF16)
    return (jnp.dot(hi, ones_blk, preferred_element_type=F32)
            + jnp.dot(lo, ones_blk, preferred_element_type=F32))


def _iota(shape, dim):
    return lax.broadcasted_iota(jnp.int32, shape, dim)


def _softplus(z):
    return jnp.maximum(z, 0.0) + jnp.log(1.0 + jnp.exp(-jnp.abs(z)))


def _modulated_norm(x, g, shift, scale):
    y = x * lax.rsqrt(jnp.mean(x * x, axis=-1, keepdims=True) + NORM_EPS) * g
    return y * (1.0 + scale) + shift


def _adaln_kernel(c_ref, w_ref, b_ref, o_ref):
    c = c_ref[...]
    o_ref[...] = _dot(c * jax.nn.sigmoid(c), w_ref[...]) + b_ref[...]


def _adaln(c_all, w_ada, b_ada):
    depth = w_ada.shape[0]
    rows = c_all.shape[0]
    tn = 1024
    nn = D_MODEL // tn
    return pl.pallas_call(
        _adaln_kernel,
        grid=(depth, N_MOD, nn),
        in_specs=[
            pl.BlockSpec((rows, D_MODEL), lambda l, m, n: (0, 0)),
            pl.BlockSpec((None, D_MODEL, tn), lambda l, m, n: (l, 0, m * nn + n)),
            pl.BlockSpec((None, 1, tn), lambda l, m, n: (l, 0, m * nn + n)),
        ],
        out_specs=pl.BlockSpec((None, None, rows, tn), lambda l, m, n: (l, m, 0, n)),
        out_shape=jax.ShapeDtypeStruct((depth, N_MOD, rows, D_MODEL), F32),
        compiler_params=pltpu.CompilerParams(
            dimension_semantics=("parallel", "parallel", "parallel"),
            vmem_limit_bytes=VMEM_LIMIT_BYTES),
        name="adaln",
    )(c_all, w_ada, b_ada.reshape(depth, 1, N_MOD * D_MODEL))


def _row_tiling(bsz, t, row_tile=ROW_TILE):
    tb = min(t, row_tile)
    assert t % tb == 0 and tb % SUBLANES == 0
    sb = 1 if tb > SUBLANES else min(bsz, ROW_TILE // tb)
    assert bsz % sb == 0
    return sb, tb, t // tb


def _mod_spec(sb, n_t, layer, m, width=D_MODEL, col=lambda j: 0):
    if sb == 1:
        return pl.BlockSpec((None, None, 1, 1, width), lambda i, j: (layer, m, i // n_t, 0, col(j)))
    return pl.BlockSpec((None, None, sb, width), lambda i, j: (layer, m, i, col(j)))


SLAB = 256


def _slabs(sb, tb):
    n = min(SLAB, sb * tb)
    assert (sb * tb) % n == 0 and (sb == 1 or n % tb == 0)
    return [(r0, n) for r0 in range(0, sb * tb, n)]


def _tok_rows(x_ref, sb, tb, r0, n):
    if sb == 1:
        return x_ref[0, r0:r0 + n, :]
    return x_ref[r0 // tb:(r0 + n) // tb].reshape(n, x_ref.shape[-1])


def _put_rows(o_ref, sb, tb, r0, n, val):
    if sb == 1:
        o_ref[0, r0:r0 + n, :] = val
    else:
        o_ref[r0 // tb:(r0 + n) // tb] = val.reshape(n // tb, tb, val.shape[-1])


def _seq_rows(m_ref, sb, tb, r0, n):
    if sb == 1:
        return m_ref[0]
    n_seq = n // tb
    seq, tok = _iota((n, n_seq), 1), _iota((n, n_seq), 0)
    expand = jnp.where(jnp.logical_and(tok >= seq * tb, tok < (seq + 1) * tb), 1.0, 0.0)
    return _dot01(expand, m_ref[r0 // tb:(r0 + n) // tb])


def _write_modulated_norm(x_ref, g_ref, shift_ref, scale_ref, h_ref):
    sb, tb, _ = x_ref.shape
    for r0, n in _slabs(sb, tb):
        h = _modulated_norm(_tok_rows(x_ref, sb, tb, r0, n), g_ref[...],
                            _seq_rows(shift_ref, sb, tb, r0, n), _seq_rows(scale_ref, sb, tb, r0, n))
        h_ref[r0:r0 + n, :] = h.astype(BF16)


def _ffn_kernel(x_ref, shift_ref, scale_ref, gate_ref, g_ref, w1a_ref, w1b_ref, w2_ref, gfin_ref,
                o_ref, h_ref, *, final_norm):
    j = pl.program_id(1)
    sb, tb, d = x_ref.shape

    @pl.when(j == 0)
    def _():
        _write_modulated_norm(x_ref, g_ref, shift_ref, scale_ref, h_ref)
        o_ref[...] = jnp.zeros_like(o_ref)

    h = h_ref[...]
    u1 = jnp.dot(h, w1a_ref[...].astype(BF16), preferred_element_type=F32)
    u2 = jnp.dot(h, w1b_ref[...].astype(BF16), preferred_element_type=F32)
    act = (u1 * jax.nn.sigmoid(u1)) * u2
    o_ref[...] += _dot(act, w2_ref[...]).reshape(sb, tb, d)

    @pl.when(j == pl.num_programs(1) - 1)
    def _():
        for r0, n in _slabs(sb, tb):
            y = (_tok_rows(x_ref, sb, tb, r0, n)
                 + 0.5 * _seq_rows(gate_ref, sb, tb, r0, n) * _tok_rows(o_ref, sb, tb, r0, n))
            if final_norm:
                y = y * lax.rsqrt(jnp.mean(y * y, axis=-1, keepdims=True) + NORM_EPS) * gfin_ref[...]
            _put_rows(o_ref, sb, tb, r0, n, y)


def _ffn(x, mods, layer, m0, g, w_in, w_out, g_final, final_norm):
    bsz, t, d = x.shape
    sb, tb, n_t = _row_tiling(bsz, t)
    tf = 512
    nf = D_FF // tf
    x_spec = pl.BlockSpec((sb, tb, d), lambda i, j: (i // n_t, i % n_t, 0), pipeline_mode=pl.Buffered(1))
    return pl.pallas_call(
        functools.partial(_ffn_kernel, final_norm=final_norm),
        grid=((bsz // sb) * n_t, nf),
        in_specs=[
            x_spec,
            _mod_spec(sb, n_t, layer, m0), _mod_spec(sb, n_t, layer, m0 + 1),
            _mod_spec(sb, n_t, layer, m0 + 2),
            pl.BlockSpec((None, 1, d), lambda i, j: (layer, 0, 0)),
            pl.BlockSpec((None, d, tf), lambda i, j: (layer, 0, j)),
            pl.BlockSpec((None, d, tf), lambda i, j: (layer, 0, j + nf)),
            pl.BlockSpec((None, tf, d), lambda i, j: (layer, j, 0)),
            pl.BlockSpec((1, d), lambda i, j: (0, 0)),
        ],
        out_specs=x_spec,
        out_shape=jax.ShapeDtypeStruct(x.shape, F32),
        scratch_shapes=[pltpu.VMEM((sb * tb, d), BF16)],
        compiler_params=pltpu.CompilerParams(
            dimension_semantics=("parallel", "arbitrary"), vmem_limit_bytes=VMEM_LIMIT_BYTES),
        name="ffn",
    )(x, mods, mods, mods, g, w_in, w_in, w_out, g_final)


def _inproj_kernel(x_ref, shift_ref, scale_ref, g_ref, w_ref, o_ref, h_ref):
    sb, tb, d = x_ref.shape

    @pl.when(pl.program_id(1) == 0)
    def _():
        _write_modulated_norm(x_ref, g_ref, shift_ref, scale_ref, h_ref)

    o_ref[...] = jnp.dot(h_ref[...], w_ref[...].astype(BF16),
                         preferred_element_type=F32).reshape(o_ref.shape)


def _inproj(x, mods, layer, g, w_in_p):
    bsz, t, d = x.shape
    sb, tb, n_t = _row_tiling(bsz, t, INPROJ_ROW_TILE)
    n_out = w_in_p.shape[-1]
    tn = 384
    assert n_out % tn == 0 and tn % LANES == 0
    return pl.pallas_call(
        _inproj_kernel,
        grid=((bsz // sb) * n_t, n_out // tn),
        in_specs=[
            pl.BlockSpec((sb, tb, d), lambda i, j: (i // n_t, i % n_t, 0),
                         pipeline_mode=pl.Buffered(1)),
            _mod_spec(sb, n_t, layer, 3), _mod_spec(sb, n_t, layer, 4),
            pl.BlockSpec((None, 1, d), lambda i, j: (layer, 0, 0)),
            pl.BlockSpec((None, d, tn), lambda i, j: (layer, 0, j)),
        ],
        out_specs=pl.BlockSpec((sb, tb, tn), lambda i, j: (i // n_t, i % n_t, j)),
        out_shape=jax.ShapeDtypeStruct((bsz, t, n_out), F32),
        scratch_shapes=[pltpu.VMEM((sb * tb, d), BF16)],
        compiler_params=pltpu.CompilerParams(
            dimension_semantics=("parallel", "arbitrary"), vmem_limit_bytes=VMEM_LIMIT_BYTES),
        name="inproj",
    )(x, mods, mods, g, w_in_p)


def _outproj_kernel(x_ref, gate_ref, og_ref, ow_ref, wg_ref, ww_ref, o_ref):
    sb, tb, tn = x_ref.shape
    og = og_ref[...].reshape(sb * tb, GLA_WIDTH)
    ow = ow_ref[...].reshape(sb * tb, RWKV_WIDTH)
    y = (jnp.dot(og, wg_ref[...].astype(BF16), preferred_element_type=F32)
         + jnp.dot(ow, ww_ref[...].astype(BF16), preferred_element_type=F32))
    o_ref[...] = (x_ref[...].reshape(sb * tb, tn)
                  + _seq_rows(gate_ref, sb, tb, 0, sb * tb) * y).reshape(sb, tb, tn)


def _outproj(x, mods, layer, o_gla, o_w, w_out):
    bsz, t, d = x.shape
    sb, tb, n_t = _row_tiling(bsz, t)
    tn = 1024
    x_spec = pl.BlockSpec((sb, tb, tn), lambda i, j: (i // n_t, i % n_t, j))
    o_spec = pl.BlockSpec((sb, tb, GLA_WIDTH), lambda i, j: (i // n_t, i % n_t, 0))
    return pl.pallas_call(
        _outproj_kernel,
        grid=((bsz // sb) * n_t, d // tn),
        in_specs=[
            x_spec,
            _mod_spec(sb, n_t, layer, 5, width=tn, col=lambda j: j),
            o_spec, o_spec,
            pl.BlockSpec((None, GLA_WIDTH, tn), lambda i, j: (layer, 0, j)),
            pl.BlockSpec((None, RWKV_WIDTH, tn), lambda i, j: (layer, 1, j)),
        ],
        out_specs=x_spec,
        out_shape=jax.ShapeDtypeStruct(x.shape, F32),
        compiler_params=pltpu.CompilerParams(
            dimension_semantics=("parallel", "arbitrary"), vmem_limit_bytes=VMEM_LIMIT_BYTES),
        name="outproj",
    )(x, mods, o_gla, o_w, w_out, w_out)


def _gla_kernel(pg_ref, s0_ref, up_ref, gb_ref, ng_ref, *rest, zero_init, sub):
    o_ref, s_ref = rest[-2:]
    c_idx = pl.program_id(1)
    nb, c, _ = pg_ref.shape
    rows = nb * c
    log_c = c.bit_length() - 1
    n_sub = c // sub

    @pl.when(c_idx == 0)
    def _():
        if zero_init:
            s_ref[...] = jnp.zeros_like(s_ref)
        else:
            s_ref[...] = s0_ref[...]

    x_a = pg_ref[:, :, PG_A0:PG_A0 + LANES].reshape(rows, LANES)
    z = _dot(x_a, up_ref[...]) + gb_ref[...]
    glog = -_softplus(-z) / GLA_GATE_NORMALIZER
    r_i, c_i = _iota((rows, rows), 0), _iota((rows, rows), 1)
    same_seq = lax.shift_right_logical(r_i, log_c) == lax.shift_right_logical(c_i, log_c)
    tril = jnp.where(jnp.logical_and(same_seq, c_i <= r_i), 1.0, 0.0)
    b_all = _dot01(tril, glog)
    causal_ss = _iota((sub, sub), 1) <= _iota((sub, sub), 0)
    lane_ss = _iota((sub, sub), 1)

    for s in range(nb):
        for h in range(GLA_HEADS):
            q = pg_ref[s, :, PG_Q0 + h * GLA_HEAD_K:PG_Q0 + (h + 1) * GLA_HEAD_K] * (GLA_HEAD_K ** -0.5)
            k = pg_ref[s, :, PG_K0 + h * GLA_HEAD_K:PG_K0 + (h + 1) * GLA_HEAD_K]
            v = pg_ref[s, :, PG_V0 + h * GLA_HEAD_V:PG_V0 + (h + 1) * GLA_HEAD_V]
            gate = pg_ref[s, :, PG_G0 + h * GLA_HEAD_V:PG_G0 + (h + 1) * GLA_HEAD_V]
            b = b_all[s * c:(s + 1) * c, h * GLA_HEAD_K:(h + 1) * GLA_HEAD_K]
            s0 = s_ref[s, h]

            o_inter = _dot(q * jnp.exp(b), s0)
            o_parts = []
            for i_sub in range(n_sub):
                r0 = i_sub * sub
                rs = slice(r0, r0 + sub)
                q_i, b_i = q[rs], b[rs]
                diag = jnp.zeros((sub, sub), F32)
                for jj in range(sub):
                    dec = jnp.exp(jnp.minimum(b_i - b[r0 + jj:r0 + jj + 1], 0.0))
                    col = jnp.sum(q_i * dec * k[r0 + jj:r0 + jj + 1], axis=-1, keepdims=True)
                    diag = jnp.where(lane_ss == jj, col, diag)
                o_i = _dot(jnp.where(causal_ss, diag, 0.0), v[rs])
                if i_sub > 0:
                    b_start = b[r0 - 1:r0]
                    q_rel = q_i * jnp.exp(b_i - b_start)
                    k_rel = k[:r0] * jnp.exp(b_start - b[:r0])
                    o_i = o_i + _dot(_dot_nt(q_rel, k_rel), v[:r0])
                o_parts.append(o_i)
            o = o_inter + (jnp.concatenate(o_parts, axis=0) if n_sub > 1 else o_parts[0])

            o = o * lax.rsqrt(jnp.mean(o * o, axis=-1, keepdims=True) + GLA_NORM_EPS) * ng_ref[...]
            o = o * (gate * jax.nn.sigmoid(gate))
            o_ref[s, :, h * GLA_HEAD_V:(h + 1) * GLA_HEAD_V] = o.astype(o_ref.dtype)

            b_last = b[c - 1:c]
            k_bar = k * jnp.exp(b_last - b)
            col = jnp.exp(jnp.broadcast_to(b_last, (GLA_HEAD_K, GLA_HEAD_K)).T)
            decay_col = jnp.concatenate([col] * (GLA_HEAD_V // GLA_HEAD_K), axis=1)
            s_ref[s, h] = s0 * decay_col + _dot_tn(k_bar, v)


def _stacked_out(layer, blk, prev):
    spec = pl.BlockSpec((None,) + blk, lambda b, c: (layer, b) + (0,) * (len(blk) - 1))
    extra_in = [] if prev is None else [prev]
    extra_specs = [] if prev is None else [pl.BlockSpec(memory_space=pl.ANY)]
    return spec, extra_in, extra_specs


def _gla(proj, s0, up_p, gk_b, norm_g, layer, chunk, zero_init, depth, prev_stack):
    bsz, t, _ = proj.shape
    n_c = t // chunk
    nb = min(bsz, LANES // chunk, GLA_SEQ_PER_BLOCK)
    assert bsz % nb == 0
    if zero_init:
        s0_arg = jnp.zeros((nb, GLA_HEADS, GLA_HEAD_K, GLA_HEAD_V), F32)
        s0_spec = pl.BlockSpec((nb, GLA_HEADS, GLA_HEAD_K, GLA_HEAD_V), lambda b, c: (0, 0, 0, 0))
    else:
        s0_arg = s0
        s0_spec = pl.BlockSpec((None, nb, GLA_HEADS, GLA_HEAD_K, GLA_HEAD_V),
                               lambda b, c: (layer, b, 0, 0, 0))
    s_spec, extra_in, extra_specs = _stacked_out(layer, (nb, GLA_HEADS, GLA_HEAD_K, GLA_HEAD_V), prev_stack)
    in_specs = [
        pl.BlockSpec((nb, chunk, SEG), lambda b, c: (b, c, 0)),
        s0_spec,
        pl.BlockSpec((None, LANES, GLA_KEY), lambda b, c: (layer, 0, 0)),
        pl.BlockSpec((None, 1, GLA_KEY), lambda b, c: (layer, 0, 0)),
        pl.BlockSpec((None, 1, GLA_HEAD_V), lambda b, c: (layer, 0, 0)),
    ] + extra_specs
    return pl.pallas_call(
        functools.partial(_gla_kernel, zero_init=zero_init, sub=min(GLA_SUB, chunk)),
        grid=(bsz // nb, n_c),
        in_specs=in_specs,
        out_specs=[
            pl.BlockSpec((nb, chunk, GLA_WIDTH), lambda b, c: (b, c, 0)),
            s_spec,
        ],
        out_shape=[
            jax.ShapeDtypeStruct((bsz, t, GLA_WIDTH), BF16),
            jax.ShapeDtypeStruct((depth, bsz, GLA_HEADS, GLA_HEAD_K, GLA_HEAD_V), F32),
        ],
        input_output_aliases={} if prev_stack is None else {len(in_specs) - 1: 1},
        compiler_params=pltpu.CompilerParams(
            dimension_semantics=("parallel", "arbitrary"), vmem_limit_bytes=VMEM_LIMIT_BYTES),
        name="gla",
    )(proj, s0_arg, up_p, gk_b, norm_g, *extra_in)


def _bdot(a, b):
    return lax.dot_general(a.astype(BF16), b.astype(BF16), (((2,), (1,)), ((0,), (0,))),
                           preferred_element_type=F32)


def _bdot_nt(a, b):
    return lax.dot_general(a.astype(BF16), b.astype(BF16), (((2,), (2,)), ((0,), (0,))),
                           preferred_element_type=F32)


def _bdot_tn(a, b):
    return lax.dot_general(a.astype(BF16), b.astype(BF16), (((1,), (1,)), ((0,), (0,))),
                           preferred_element_type=F32)


def _pair_tiles(s_heads):
    return jnp.where(_pair_mask(), jnp.concatenate([s_heads, s_heads], axis=-1), 0.0)


def _pair_heads(tiles):
    first = _iota((LANES, RWKV_HEAD), 0) < RWKV_HEAD
    return jnp.where(first, tiles[:, :, :RWKV_HEAD], tiles[:, :, RWKV_HEAD:])


def _pair_mask():
    return (_iota((LANES, LANES), 0) < RWKV_HEAD) == (_iota((LANES, LANES), 1) < RWKV_HEAD)


def _rwkv_kernel(pr_ref, sh0_ref, s0_ref, mu_ref, w0_ref, w2_ref, a0_ref, a2_ref, g2_ref, kk_ref,
                 ka_ref, rk_ref, lnw_ref, lnb_ref, *rest, zero_init):
    o_ref, s_out_ref, sh_ref, s_ref = rest[-4:]
    c_idx = pl.program_id(1)
    nb, c, _ = pr_ref.shape
    rows = nb * c
    log_c = c.bit_length() - 1
    r2 = 2 * c
    g_all = nb * N_PAIR

    @pl.when(c_idx == 0)
    def _():
        sh_ref[...] = sh0_ref[...]
        if zero_init:
            s_ref[...] = jnp.zeros_like(s_ref)
        else:
            s_ref[...] = _pair_tiles(s0_ref[...].reshape(g_all, LANES, RWKV_HEAD)).reshape(s_ref.shape)

    pr3 = pr_ref[...]
    pr = pr3.reshape(rows, SEG)
    prev_rows = jnp.broadcast_to(sh_ref[...], (nb, c, SEG)).reshape(rows, SEG)
    first_tok = (_iota((rows, 1), 0) & (c - 1)) == 0
    pr_prev = jnp.where(first_tok, prev_rows, pltpu.roll(pr, 1, 0))
    sh_ref[...] = pr3[:, c - 1:c, :]
    xr = pr + (pr_prev - pr) * mu_ref[...]

    r = xr[:, PR_R0:PR_R0 + RWKV_WIDTH]
    kr = xr[:, PR_K0:PR_K0 + RWKV_WIDTH]
    vr = xr[:, PR_V0:PR_V0 + RWKV_WIDTH]
    x_wa = xr[:, PR_WA0:PR_WA0 + LANES]
    x_g = xr[:, PR_G0:PR_G0 + PR_GW]
    w_inner = -_softplus(-(w0_ref[...] + _dot(jnp.tanh(x_wa), w2_ref[...]))) - 0.5
    log_w = -jnp.exp(w_inner)
    a = jax.nn.sigmoid(a0_ref[...] + _dot(x_wa, a2_ref[...]))
    gate = _dot(jax.nn.sigmoid(x_g), g2_ref[...])
    k_mod = kr * (1.0 + (a - 1.0) * ka_ref[...])
    kk_raw = kr * kk_ref[...]

    ones_blk = ((_iota((LANES, LANES), 0) < RWKV_HEAD) == (_iota((LANES, LANES), 1) < RWKV_HEAD)).astype(BF16)

    def head_sums(x):
        return jnp.concatenate([_segsum(x[:, j * LANES:(j + 1) * LANES], ones_blk)
                                for j in range(RWKV_WIDTH // LANES)], axis=1)

    kk = kk_raw / jnp.maximum(jnp.sqrt(head_sums(kk_raw * kk_raw)), 1e-12)
    beta = kk * a

    seq_r = lax.shift_right_logical(_iota((rows, rows), 0), log_c)
    seq_c = lax.shift_right_logical(_iota((rows, rows), 1), log_c)
    same_seq = seq_r == seq_c
    tril = jnp.where(jnp.logical_and(same_seq, _iota((rows, rows), 1) <= _iota((rows, rows), 0)), 1.0, 0.0)
    cum = _dot01(tril, log_w)
    cum_ex = cum - log_w
    cum_last = jnp.concatenate(
        [jnp.broadcast_to(cum[(s + 1) * c - 1:(s + 1) * c], (c, RWKV_WIDTH)) for s in range(nb)], axis=0)
    e_neg = jnp.exp(-cum)
    e_last = jnp.exp(cum_last - cum)

    head_mask = (_iota((r2, LANES), 0) < c) == (_iota((r2, LANES), 1) < RWKV_HEAD)

    def groups(m):
        parts = []
        for s in range(nb):
            for p in range(N_PAIR):
                blk = m[s * c:(s + 1) * c, p * LANES:(p + 1) * LANES]
                parts.append(jnp.where(head_mask, jnp.concatenate([blk, blk], axis=0), 0.0))
        return jnp.stack(parts, axis=0)

    al_s = groups(-kk * jnp.exp(cum_ex))
    rt_s = groups(r * jnp.exp(cum))
    bh_s = groups(beta * e_neg)
    kh_s = groups(k_mod * e_neg)
    v_s = groups(vr)
    bb_s = groups(beta * e_last)
    kb_s = groups(k_mod * e_last)

    row2 = _iota((r2, r2), 0)
    col2 = _iota((r2, r2), 1)
    same_head = (row2 < c) == (col2 < c)
    lower_strict = jnp.logical_and(same_head, col2 < row2)
    lower_incl = jnp.logical_and(same_head, col2 <= row2)
    eye2 = jnp.where(row2 == col2, 1.0, 0.0)

    if r2 % LANES == 0:
        quad = _bdot_nt(jnp.concatenate([al_s, rt_s], axis=1), jnp.concatenate([bh_s, kh_s], axis=1))
        a_b = jnp.where(lower_strict, quad[:, :r2, :r2], 0.0)
        a_k = jnp.where(lower_strict, quad[:, :r2, r2:], 0.0)
        b_bk = jnp.where(jnp.concatenate([lower_incl, lower_incl], axis=1), quad[:, r2:, :], 0.0)
        out_intra = lambda u: _bdot(b_bk, jnp.concatenate([u, v_s], axis=1))
    else:
        a_b = jnp.where(lower_strict, _bdot_nt(al_s, bh_s), 0.0)
        a_k = jnp.where(lower_strict, _bdot_nt(al_s, kh_s), 0.0)
        b_b = jnp.where(lower_incl, _bdot_nt(rt_s, bh_s), 0.0)
        b_k = jnp.where(lower_incl, _bdot_nt(rt_s, kh_s), 0.0)
        out_intra = lambda u: _bdot(b_b, u) + _bdot(b_k, v_s)
    t_inv = eye2 + a_b
    pw = a_b
    for _ in range(max(log_c - 1, 0)):
        pw = _bdot(pw, pw)
        t_inv = t_inv + _bdot(t_inv, pw)

    s0 = s_ref[...].reshape(g_all, LANES, LANES)
    u_s = _bdot(t_inv, _bdot(a_k, v_s) + _bdot_nt(al_s, s0))
    o_s = _bdot_nt(rt_s, s0) + out_intra(u_s)

    decay = jnp.stack([jnp.exp(cum_last[s * c:s * c + 1, p * LANES:(p + 1) * LANES])
                       for s in range(nb) for p in range(N_PAIR)], axis=0)
    s_new = s0 * decay + _bdot_tn(jnp.concatenate([u_s, v_s], axis=1), jnp.concatenate([bb_s, kb_s], axis=1))
    s_ref[...] = s_new.reshape(s_ref.shape)

    @pl.when(c_idx == pl.num_programs(1) - 1)
    def _():
        s_out_ref[...] = _pair_heads(s_new).reshape(s_out_ref.shape)

    o_g = o_s[:, :c, :] + o_s[:, c:, :]
    o2 = jnp.concatenate([jnp.concatenate([o_g[s * N_PAIR + p] for p in range(N_PAIR)], axis=1)
                          for s in range(nb)], axis=0)

    mean = head_sums(o2) * (1.0 / RWKV_HEAD)
    cen = o2 - mean
    var = head_sums(cen * cen) * (1.0 / RWKV_HEAD)
    y = cen * lax.rsqrt(var + RWKV_GN_EPS) * lnw_ref[...] + lnb_ref[...]
    y = y + head_sums(r * k_mod * rk_ref[...]) * vr
    o_ref[...] = (y * gate).reshape(nb, c, RWKV_WIDTH).astype(o_ref.dtype)


def _rwkv(proj, shift0, s0, wts, layer, chunk, zero_init, depth, prev_stack):
    bsz, t, _ = proj.shape
    n_c = t // chunk
    nb = min(bsz, LANES // chunk, RWKV_SEQ_PER_BLOCK)
    assert bsz % nb == 0 and LANES % chunk == 0
    head_blk = (nb, RWKV_HEADS, RWKV_HEAD, RWKV_HEAD)
    if zero_init:
        s0_arg = jnp.zeros(head_blk, F32)
        s0_spec = pl.BlockSpec(head_blk, lambda b, c: (0, 0, 0, 0))
    else:
        s0_arg = s0
        s0_spec = pl.BlockSpec((None,) + head_blk, lambda b, c: (layer, b, 0, 0, 0))
    s_spec, extra_in, extra_specs = _stacked_out(layer, head_blk, prev_stack)
    vec = lambda n: pl.BlockSpec((None, 1, n), lambda b, c: (layer, 0, 0))
    mat = lambda m, n: pl.BlockSpec((None, m, n), lambda b, c: (layer, 0, 0))
    in_specs = [
        pl.BlockSpec((nb, chunk, SEG), lambda b, c: (b, c, 1)),
        pl.BlockSpec((None, nb, 1, SEG), lambda b, c: (layer, b, 0, 0)),
        s0_spec,
        vec(SEG), vec(RWKV_WIDTH), mat(LANES, RWKV_WIDTH), vec(RWKV_WIDTH),
        mat(LANES, RWKV_WIDTH), mat(PR_GW, RWKV_WIDTH), vec(RWKV_WIDTH), vec(RWKV_WIDTH),
        vec(RWKV_WIDTH), vec(RWKV_WIDTH), vec(RWKV_WIDTH),
    ] + extra_specs
    return pl.pallas_call(
        functools.partial(_rwkv_kernel, zero_init=zero_init),
        grid=(bsz // nb, n_c),
        in_specs=in_specs,
        out_specs=[
            pl.BlockSpec((nb, chunk, RWKV_WIDTH), lambda b, c: (b, c, 0)),
            s_spec,
            pl.BlockSpec((nb, 1, SEG), lambda b, c: (b, 0, 0)),
        ],
        out_shape=[
            jax.ShapeDtypeStruct((bsz, t, RWKV_WIDTH), BF16),
            jax.ShapeDtypeStruct((depth, bsz, RWKV_HEADS, RWKV_HEAD, RWKV_HEAD), F32),
            jax.ShapeDtypeStruct((bsz, 1, SEG), F32),
        ],
        input_output_aliases={} if prev_stack is None else {len(in_specs) - 1: 1},
        scratch_shapes=[pltpu.VMEM((nb, N_PAIR, LANES, LANES), F32)],
        compiler_params=pltpu.CompilerParams(
            dimension_semantics=("parallel", "arbitrary"), vmem_limit_bytes=VMEM_LIMIT_BYTES),
        name="rwkv",
    )(proj, shift0, s0_arg, *wts, *extra_in)


def _pad_rows(w, rows_before, rows_total):
    depth, r, n = w.shape
    return jnp.concatenate([jnp.zeros((depth, rows_before, n), w.dtype), w,
                            jnp.zeros((depth, rows_total - rows_before - r, n), w.dtype)], axis=1)


def _trunk(x, mods, state_gla, state_wkv, state_shift_p, params, zero_init):
    (g_ffn1, w_ffn1_in, w_ffn1_out, g_mix, w_in_p, up_p, gk_b, norm_g, rwkv_wts, w_out, g_ffn2,
     w_ffn2_in, w_ffn2_out, g_final) = params
    depth = w_in_p.shape[0]
    bsz, t, _ = x.shape
    gla_chunk = min(GLA_CHUNK, t)
    rwkv_chunk = min(RWKV_CHUNK, t)
    new_gla, new_wkv, new_shift = None, None, []
    for layer in range(depth):
        x = _ffn(x, mods, layer, 0, g_ffn1, w_ffn1_in, w_ffn1_out, g_final, False)
        proj = _inproj(x, mods, layer, g_mix, w_in_p)
        o_gla, new_gla = _gla(proj, state_gla, up_p, gk_b, norm_g, layer, gla_chunk, zero_init, depth,
                              new_gla)
        o_w, new_wkv, shift = _rwkv(proj, state_shift_p, state_wkv, rwkv_wts, layer, rwkv_chunk,
                                    zero_init, depth, new_wkv)
        x = _outproj(x, mods, layer, o_gla, o_w, w_out)
        x = _ffn(x, mods, layer, 6, g_ffn2, w_ffn2_in, w_ffn2_out, g_final, layer == depth - 1)
        new_shift.append(shift[:, 0, :RWKV_IN])
    return x, new_gla, new_wkv, jnp.stack(new_shift)


def kernel(x_prompt, x_sample, c_prompt, c_sample, state_gla, state_wkv, state_shift, w_ada, b_ada, g_ffn1, w_ffn1_in, w_ffn1_out, g_mix, w_in, gla_gk_up, gla_gk_b, gla_norm_g, rwkv_mu, rwkv_w0, rwkv_w2, rwkv_a0, rwkv_a2, rwkv_g2, rwkv_k_k, rwkv_k_a, rwkv_r_k, rwkv_ln_w, rwkv_ln_b, w_out, g_ffn2, w_ffn2_in, w_ffn2_out, g_final):
    depth = w_in.shape[0]
    bp, bs = x_prompt.shape[0], x_sample.shape[0]
    assert x_prompt.shape[-1] == D_MODEL and w_in.shape[-1] == GLA_IN + RWKV_IN

    n_seq = bp + bs
    n_seq_p = -(-n_seq // SUBLANES) * SUBLANES
    c_all = jnp.concatenate([c_sample, c_prompt, jnp.zeros((n_seq_p - n_seq, D_MODEL), F32)], axis=0)
    mods = _adaln(c_all, w_ada, b_ada)
    mods_p = mods[:, :, bs:n_seq, None, :]

    zcol = lambda n: jnp.zeros((depth, D_MODEL, n), w_in.dtype)
    a0 = 2 * GLA_KEY + GLA_WIDTH
    w_in_p = jnp.concatenate([
        w_in[:, :, :a0], w_in[:, :, a0 + GLA_GATE_RANK:GLA_IN], w_in[:, :, a0:a0 + GLA_GATE_RANK],
        zcol(SEG - GLA_IN), w_in[:, :, GLA_IN:], zcol(SEG - RWKV_IN)], axis=-1)
    row = lambda v: v.reshape(depth, 1, -1)
    pad_seg = lambda v: jnp.pad(v, [(0, 0)] * (v.ndim - 1) + [(0, SEG - RWKV_IN)])
    up_p = _pad_rows(gla_gk_up, 0, LANES)
    rwkv_wts = (
        row(pad_seg(rwkv_mu)), row(rwkv_w0), _pad_rows(rwkv_w2, 0, LANES), row(rwkv_a0),
        _pad_rows(rwkv_a2, RWKV_W_LORA, LANES), _pad_rows(rwkv_g2, 0, PR_GW), row(rwkv_k_k),
        row(rwkv_k_a), row(rwkv_r_k), row(rwkv_ln_w), row(rwkv_ln_b))
    params = (row(g_ffn1), w_ffn1_in, w_ffn1_out, row(g_mix), w_in_p, up_p, row(gla_gk_b),
              row(gla_norm_g), rwkv_wts, w_out, row(g_ffn2), w_ffn2_in, w_ffn2_out,
              g_final.reshape(1, D_MODEL))

    shift_zero = jnp.zeros((depth, bp, 1, SEG), F32)
    y_p, gla_p, wkv_p, shift_p = _trunk(x_prompt, mods_p, None, None, shift_zero, params, True)
    shift_s0 = pad_seg(state_shift)[:, :, None, :]
    y_s, gla_s, wkv_s, shift_s = _trunk(x_sample, mods, state_gla, state_wkv, shift_s0, params, False)
    return (y_p, y_s, gla_p, wkv_p, shift_p, gla_s, wkv_s, shift_s)
```

```python
import functools

import jax
import jax.numpy as jnp
from jax import lax
from jax.experimental import pallas as pl
from jax.experimental.pallas import tpu as pltpu

F32 = jnp.float32
BF16 = jnp.bfloat16

D_MODEL = 2048
D_FF = 5632
N_MOD = 9
NORM_EPS = 1e-6
GLA_HEADS = 4
GLA_HEAD_K = 128
GLA_HEAD_V = 256
GLA_KEY = GLA_HEADS * GLA_HEAD_K
GLA_WIDTH = GLA_HEADS * GLA_HEAD_V
GLA_GATE_RANK = 16
GLA_GATE_NORMALIZER = 16.0
GLA_NORM_EPS = 1e-5
GLA_IN = 2 * GLA_KEY + 2 * GLA_WIDTH + GLA_GATE_RANK
RWKV_HEAD = 64
RWKV_HEADS = 16
RWKV_WIDTH = RWKV_HEADS * RWKV_HEAD
RWKV_W_LORA = 64
RWKV_A_LORA = 64
RWKV_G_LORA = 160
RWKV_IN = 3 * RWKV_WIDTH + RWKV_W_LORA + RWKV_A_LORA + RWKV_G_LORA
RWKV_GN_EPS = 64e-5

LANES = 128
SUBLANES = 8
VMEM_LIMIT_BYTES = 60 * 1024 * 1024

PROJ_TILE = 512
SEG = 7 * PROJ_TILE
N_PROJ_TILES = -(-(GLA_IN + RWKV_IN) // PROJ_TILE)
PR_WIN0 = (N_PROJ_TILES - 7) * PROJ_TILE
PR_LEAD = GLA_IN - PR_WIN0
PG_Q0, PG_K0, PG_V0 = 0, GLA_KEY, 2 * GLA_KEY
PG_A0 = PG_V0 + GLA_WIDTH
PR_R0, PR_K0, PR_V0, PR_WA0, PR_G0 = 0, 1024, 2048, 3072, 3200
PR_GW = 256
N_PAIR = RWKV_HEADS // 2

ROW_TILE = 1024
INPROJ_ROW_TILE = 1024
GLA_CHUNK = 64
GLA_SUB = 16
GLA_SEQ_PER_BLOCK = 8
RWKV_CHUNK = 64
RWKV_SEQ_PER_BLOCK = 8


def _dot(a, b):
    return jnp.dot(a.astype(BF16), b.astype(BF16), preferred_element_type=F32)


def _dot_nt(a, b):
    return lax.dot_general(a.astype(BF16), b.astype(BF16), (((1,), (1,)), ((), ())),
                           preferred_element_type=F32)


def _dot_tn(a, b):
    return lax.dot_general(a.astype(BF16), b.astype(BF16), (((0,), (0,)), ((), ())),
                           preferred_element_type=F32)


def _split3(x):
    hi = x.astype(BF16)
    r1 = x - hi.astype(F32)
    mid = r1.astype(BF16)
    lo = (r1 - mid.astype(F32)).astype(BF16)
    return hi, mid, lo


def _dot01(m01, x):
    m = m01.astype(BF16)
    hi, mid, lo = _split3(x)
    f = lambda t: jnp.dot(m, t, preferred_element_type=F32)
    return f(hi) + f(mid) + f(lo)


def _segsum(x, ones_blk):
    hi = x.astype(BF16)
    lo = (x - hi.astype(F32)).astype(BF16)
    return (jnp.dot(hi, ones_blk, preferred_element_type=F32)
            + jnp.dot(lo, ones_blk, preferred_element_type=F32))


def _iota(shape, dim):
    return lax.broadcasted_iota(jnp.int32, shape, dim)


def _softplus(z):
    return jnp.maximum(z, 0.0) + jnp.log(1.0 + jnp.exp(-jnp.abs(z)))


def _modulated_norm(x, g, shift, scale):
    y = x * lax.rsqrt(jnp.mean(x * x, axis=-1, keepdims=True) + NORM_EPS) * g
    return y * (1.0 + scale) + shift


def _adaln_kernel(c_ref, w_ref, b_ref, o_ref):
    c = c_ref[...]
    o_ref[...] = _dot(c * jax.nn.sigmoid(c), w_ref[...]) + b_ref[...]


def _adaln(c_all, w_ada, b_ada):
    depth = w_ada.shape[0]
    rows = c_all.shape[0]
    tn = 1024
    nn = D_MODEL // tn
    return pl.pallas_call(
        _adaln_kernel,
        grid=(depth, N_MOD, nn),
        in_specs=[
            pl.BlockSpec((rows, D_MODEL), lambda l, m, n: (0, 0)),
            pl.BlockSpec((None, D_MODEL, tn), lambda l, m, n: (l, 0, m * nn + n)),
            pl.BlockSpec((None, 1, tn), lambda l, m, n: (l, 0, m * nn + n)),
        ],
        out_specs=pl.BlockSpec((None, None, rows, tn), lambda l, m, n: (l, m, 0, n)),
        out_shape=jax.ShapeDtypeStruct((depth, N_MOD, rows, D_MODEL), F32),
        compiler_params=pltpu.CompilerParams(
            dimension_semantics=("parallel", "parallel", "parallel"),
            vmem_limit_bytes=VMEM_LIMIT_BYTES),
        name="adaln",
    )(c_all, w_ada, b_ada.reshape(depth, 1, N_MOD * D_MODEL))


def _row_tiling(bsz, t, row_tile=ROW_TILE):
    tb = min(t, row_tile)
    assert t % tb == 0 and tb % SUBLANES == 0
    sb = 1 if tb > SUBLANES else min(bsz, ROW_TILE // tb)
    assert bsz % sb == 0
    return sb, tb, t // tb


def _mod_spec(sb, n_t, layer, m, width=D_MODEL, col=lambda j: 0):
    if sb == 1:
        return pl.BlockSpec((None, None, 1, 1, width), lambda i, j: (layer, m, i // n_t, 0, col(j)))
    return pl.BlockSpec((None, None, sb, width), lambda i, j: (layer, m, i, col(j)))


SLAB = 256


def _slabs(sb, tb):
    n = min(SLAB, sb * tb)
    assert (sb * tb) % n == 0 and (sb == 1 or n % tb == 0)
    return [(r0, n) for r0 in range(0, sb * tb, n)]


def _tok_rows(x_ref, sb, tb, r0, n):
    if sb == 1:
        return x_ref[0, r0:r0 + n, :]
    return x_ref[r0 // tb:(r0 + n) // tb].reshape(n, x_ref.shape[-1])


def _put_rows(o_ref, sb, tb, r0, n, val):
    if sb == 1:
        o_ref[0, r0:r0 + n, :] = val
    else:
        o_ref[r0 // tb:(r0 + n) // tb] = val.reshape(n // tb, tb, val.shape[-1])


def _seq_rows(m_ref, sb, tb, r0, n):
    if sb == 1:
        return m_ref[0]
    n_seq = n // tb
    seq, tok = _iota((n, n_seq), 1), _iota((n, n_seq), 0)
    expand = jnp.where(jnp.logical_and(tok >= seq * tb, tok < (seq + 1) * tb), 1.0, 0.0)
    return _dot01(expand, m_ref[r0 // tb:(r0 + n) // tb])


def _write_modulated_norm(x_ref, g_ref, shift_ref, scale_ref, h_ref):
    sb, tb, _ = x_ref.shape
    for r0, n in _slabs(sb, tb):
        h = _modulated_norm(_tok_rows(x_ref, sb, tb, r0, n), g_ref[...],
                            _seq_rows(shift_ref, sb, tb, r0, n), _seq_rows(scale_ref, sb, tb, r0, n))
        h_ref[r0:r0 + n, :] = h.astype(BF16)


def _ffn_kernel(x_ref, shift_ref, scale_ref, gate_ref, g_ref, w1a_ref, w1b_ref, w2_ref, gfin_ref,
                o_ref, h_ref, *, final_norm):
    j = pl.program_id(1)
    sb, tb, d = x_ref.shape

    @pl.when(j == 0)
    def _():
        _write_modulated_norm(x_ref, g_ref, shift_ref, scale_ref, h_ref)
        o_ref[...] = jnp.zeros_like(o_ref)

    h = h_ref[...]
    u1 = jnp.dot(h, w1a_ref[...].astype(BF16), preferred_element_type=F32)
    u2 = jnp.dot(h, w1b_ref[...].astype(BF16), preferred_element_type=F32)
    act = (u1 * jax.nn.sigmoid(u1)) * u2
    o_ref[...] += _dot(act, w2_ref[...]).reshape(sb, tb, d)

    @pl.when(j == pl.num_programs(1) - 1)
    def _():
        for r0, n in _slabs(sb, tb):
            y = (_tok_rows(x_ref, sb, tb, r0, n)
                 + 0.5 * _seq_rows(gate_ref, sb, tb, r0, n) * _tok_rows(o_ref, sb, tb, r0, n))
            if final_norm:
                y = y * lax.rsqrt(jnp.mean(y * y, axis=-1, keepdims=True) + NORM_EPS) * gfin_ref[...]
            _put_rows(o_ref, sb, tb, r0, n, y)


def _ffn(x, mods, layer, m0, g, w_in, w_out, g_final, final_norm):
    bsz, t, d = x.shape
    sb, tb, n_t = _row_tiling(bsz, t)
    tf = 512
    nf = D_FF // tf
    x_spec = pl.BlockSpec((sb, tb, d), lambda i, j: (i // n_t, i % n_t, 0), pipeline_mode=pl.Buffered(1))
    return pl.pallas_call(
        functools.partial(_ffn_kernel, final_norm=final_norm),
        grid=((bsz // sb) * n_t, nf),
        in_specs=[
            x_spec,
            _mod_spec(sb, n_t, layer, m0), _mod_spec(sb, n_t, layer, m0 + 1),
            _mod_spec(sb, n_t, layer, m0 + 2),
            pl.BlockSpec((None, 1, d), lambda i, j: (layer, 0, 0)),
            pl.BlockSpec((None, d, tf), lambda i, j: (layer, 0, j)),
            pl.BlockSpec((None, d, tf), lambda i, j: (layer, 0, j + nf)),
            pl.BlockSpec((None, tf, d), lambda i, j: (layer, j, 0)),
            pl.BlockSpec((1, d), lambda i, j: (0, 0)),
        ],
        out_specs=x_spec,
        out_shape=jax.ShapeDtypeStruct(x.shape, F32),
        scratch_shapes=[pltpu.VMEM((sb * tb, d), BF16)],
        compiler_params=pltpu.CompilerParams(
            dimension_semantics=("parallel", "arbitrary"), vmem_limit_bytes=VMEM_LIMIT_BYTES),
        name="ffn",
    )(x, mods, mods, mods, g, w_in, w_in, w_out, g_final)


def _inproj_kernel(x_ref, shift_ref, scale_ref, g_ref, wt_ref, wtail_ref, og_ref, or_ref, h_ref):
    j = pl.program_id(1)
    shared = SEG // PROJ_TILE - 1
    last = N_PROJ_TILES - 1

    @pl.when(j == 0)
    def _():
        _write_modulated_norm(x_ref, g_ref, shift_ref, scale_ref, h_ref)

    def tile(w_ref):
        return lax.dot_general(h_ref[...], w_ref[...].astype(BF16), (((1,), (1,)), ((), ())),
                               preferred_element_type=F32)

    @pl.when(j < shared)
    def _():
        og_ref[...] = tile(wt_ref).reshape(og_ref.shape)

    @pl.when(j == shared)
    def _():
        both = tile(wt_ref)
        og_ref[...] = both.reshape(og_ref.shape)
        or_ref[...] = both.reshape(or_ref.shape)

    @pl.when(jnp.logical_and(j > shared, j < last))
    def _():
        or_ref[...] = tile(wt_ref).reshape(or_ref.shape)

    @pl.when(j == last)
    def _():
        or_ref[...] = tile(wtail_ref).reshape(or_ref.shape)


def _inproj(x, mods, layer, g, w_in_t, w_tail):
    bsz, t, d = x.shape
    sb, tb, n_t = _row_tiling(bsz, t, INPROJ_ROW_TILE)
    shared = SEG // PROJ_TILE - 1
    last = N_PROJ_TILES - 1
    row_map = lambda i: (i // n_t, i % n_t)
    return pl.pallas_call(
        _inproj_kernel,
        grid=((bsz // sb) * n_t, N_PROJ_TILES),
        in_specs=[
            pl.BlockSpec((sb, tb, d), lambda i, j: row_map(i) + (0,),
                         pipeline_mode=pl.Buffered(1)),
            _mod_spec(sb, n_t, layer, 3), _mod_spec(sb, n_t, layer, 4),
            pl.BlockSpec((None, 1, d), lambda i, j: (layer, 0, 0)),
            pl.BlockSpec((None, PROJ_TILE, d), lambda i, j: (layer, jnp.minimum(j, last - 1), 0)),
            pl.BlockSpec((None, PROJ_TILE, d), lambda i, j: (layer, 0, 0), pipeline_mode=pl.Buffered(1)),
        ],
        out_specs=[
            pl.BlockSpec((sb, tb, PROJ_TILE), lambda i, j: row_map(i) + (jnp.minimum(j, shared),)),
            pl.BlockSpec((sb, tb, PROJ_TILE), lambda i, j: row_map(i) + (jnp.maximum(j - shared, 0),)),
        ],
        out_shape=[jax.ShapeDtypeStruct((bsz, t, SEG), F32), jax.ShapeDtypeStruct((bsz, t, SEG), F32)],
        scratch_shapes=[pltpu.VMEM((sb * tb, d), BF16)],
        compiler_params=pltpu.CompilerParams(
            dimension_semantics=("parallel", "arbitrary"), vmem_limit_bytes=VMEM_LIMIT_BYTES),
        name="inproj",
    )(x, mods, mods, g, w_in_t, w_tail)


def _outproj_kernel(x_ref, gate_ref, og_ref, ow_ref, wg_ref, ww_ref, o_ref):
    sb, tb, tn = x_ref.shape
    og = og_ref[...].reshape(sb * tb, GLA_WIDTH)
    ow = ow_ref[...].reshape(sb * tb, RWKV_WIDTH)
    y = (jnp.dot(og, wg_ref[...].astype(BF16), preferred_element_type=F32)
         + jnp.dot(ow, ww_ref[...].astype(BF16), preferred_element_type=F32))
    o_ref[...] = (x_ref[...].reshape(sb * tb, tn)
                  + _seq_rows(gate_ref, sb, tb, 0, sb * tb) * y).reshape(sb, tb, tn)


def _outproj(x, mods, layer, o_gla, o_w, w_out):
    bsz, t, d = x.shape
    sb, tb, n_t = _row_tiling(bsz, t)
    tn = 1024
    x_spec = pl.BlockSpec((sb, tb, tn), lambda i, j: (i // n_t, i % n_t, j))
    o_spec = pl.BlockSpec((sb, tb, GLA_WIDTH), lambda i, j: (i // n_t, i % n_t, 0))
    return pl.pallas_call(
        _outproj_kernel,
        grid=((bsz // sb) * n_t, d // tn),
        in_specs=[
            x_spec,
            _mod_spec(sb, n_t, layer, 5, width=tn, col=lambda j: j),
            o_spec, o_spec,
            pl.BlockSpec((None, GLA_WIDTH, tn), lambda i, j: (layer, 0, j)),
            pl.BlockSpec((None, RWKV_WIDTH, tn), lambda i, j: (layer, 1, j)),
        ],
        out_specs=x_spec,
        out_shape=jax.ShapeDtypeStruct(x.shape, F32),
        compiler_params=pltpu.CompilerParams(
            dimension_semantics=("parallel", "arbitrary"), vmem_limit_bytes=VMEM_LIMIT_BYTES),
        name="outproj",
    )(x, mods, o_gla, o_w, w_out, w_out)


def _gla_kernel(pg_ref, s0_ref, up_ref, gb_ref, ng_ref, stack_ref, o_ref, s_ref, *, zero_init, sub):
    del stack_ref
    c_idx = pl.program_id(1)
    nb, c, _ = pg_ref.shape
    rows = nb * c
    log_c = c.bit_length() - 1
    n_sub = c // sub

    @pl.when(c_idx == 0)
    def _():
        if zero_init:
            s_ref[...] = jnp.zeros_like(s_ref)
        else:
            s_ref[...] = s0_ref[...]

    x_a = pg_ref[:, :, PG_A0:PG_A0 + LANES].reshape(rows, LANES)
    z = _dot(x_a, up_ref[...]) + gb_ref[...]
    gate_w = GLA_WIDTH + LANES
    gates = pltpu.roll(pg_ref[:, :, PG_A0:PG_A0 + gate_w].reshape(rows, gate_w),
                       gate_w - GLA_GATE_RANK, 1)
    glog = -_softplus(-z) / GLA_GATE_NORMALIZER
    r_i, c_i = _iota((rows, rows), 0), _iota((rows, rows), 1)
    same_seq = lax.shift_right_logical(r_i, log_c) == lax.shift_right_logical(c_i, log_c)
    tril = jnp.where(jnp.logical_and(same_seq, c_i <= r_i), 1.0, 0.0)
    b_all = _dot01(tril, glog)
    causal_ss = _iota((sub, sub), 1) <= _iota((sub, sub), 0)
    lane_ss = _iota((sub, sub), 1)

    for s in range(nb):
        for h in range(GLA_HEADS):
            q = pg_ref[s, :, PG_Q0 + h * GLA_HEAD_K:PG_Q0 + (h + 1) * GLA_HEAD_K] * (GLA_HEAD_K ** -0.5)
            k = pg_ref[s, :, PG_K0 + h * GLA_HEAD_K:PG_K0 + (h + 1) * GLA_HEAD_K]
            v = pg_ref[s, :, PG_V0 + h * GLA_HEAD_V:PG_V0 + (h + 1) * GLA_HEAD_V]
            gate = gates[s * c:(s + 1) * c, h * GLA_HEAD_V:(h + 1) * GLA_HEAD_V]
            b = b_all[s * c:(s + 1) * c, h * GLA_HEAD_K:(h + 1) * GLA_HEAD_K]
            s0 = s_ref[s, h]

            o_inter = _dot(q * jnp.exp(b), s0)
            o_parts = []
            for i_sub in range(n_sub):
                r0 = i_sub * sub
                rs = slice(r0, r0 + sub)
                q_i, b_i = q[rs], b[rs]
                diag = jnp.zeros((sub, sub), F32)
                for jj in range(sub):
                    dec = jnp.exp(jnp.minimum(b_i - b[r0 + jj:r0 + jj + 1], 0.0))
                    col = jnp.sum(q_i * dec * k[r0 + jj:r0 + jj + 1], axis=-1, keepdims=True)
                    diag = jnp.where(lane_ss == jj, col, diag)
                o_i = _dot(jnp.where(causal_ss, diag, 0.0), v[rs])
                if i_sub > 0:
                    b_start = b[r0 - 1:r0]
                    q_rel = q_i * jnp.exp(b_i - b_start)
                    k_rel = k[:r0] * jnp.exp(b_start - b[:r0])
                    o_i = o_i + _dot(_dot_nt(q_rel, k_rel), v[:r0])
                o_parts.append(o_i)
            o = o_inter + (jnp.concatenate(o_parts, axis=0) if n_sub > 1 else o_parts[0])

            o = o * lax.rsqrt(jnp.mean(o * o, axis=-1, keepdims=True) + GLA_NORM_EPS) * ng_ref[...]
            o = o * (gate * jax.nn.sigmoid(gate))
            o_ref[s, :, h * GLA_HEAD_V:(h + 1) * GLA_HEAD_V] = o.astype(o_ref.dtype)

            b_last = b[c - 1:c]
            k_bar = k * jnp.exp(b_last - b)
            col = jnp.exp(jnp.broadcast_to(b_last, (GLA_HEAD_K, GLA_HEAD_K)).T)
            decay_col = jnp.concatenate([col] * (GLA_HEAD_V // GLA_HEAD_K), axis=1)
            s_ref[s, h] = s0 * decay_col + _dot_tn(k_bar, v)


def _stack_slice_spec(layer, blk):
    return pl.BlockSpec((None,) + blk, lambda b, c: (layer, b) + (0,) * (len(blk) - 1))


def _gla(proj, s0, up_p, gk_b, norm_g, layer, chunk, zero_init, stack):
    bsz, t, _ = proj.shape
    n_c = t // chunk
    nb = min(bsz, LANES // chunk, GLA_SEQ_PER_BLOCK)
    assert bsz % nb == 0
    if zero_init:
        s0_arg = jnp.zeros((nb, GLA_HEADS, GLA_HEAD_K, GLA_HEAD_V), F32)
        s0_spec = pl.BlockSpec((nb, GLA_HEADS, GLA_HEAD_K, GLA_HEAD_V), lambda b, c: (0, 0, 0, 0))
    else:
        s0_arg = s0
        s0_spec = pl.BlockSpec((None, nb, GLA_HEADS, GLA_HEAD_K, GLA_HEAD_V),
                               lambda b, c: (layer, b, 0, 0, 0))
    in_specs = [
        pl.BlockSpec((nb, chunk, SEG), lambda b, c: (b, c, 0)),
        s0_spec,
        pl.BlockSpec((None, LANES, GLA_KEY), lambda b, c: (layer, 0, 0)),
        pl.BlockSpec((None, 1, GLA_KEY), lambda b, c: (layer, 0, 0)),
        pl.BlockSpec((None, 1, GLA_HEAD_V), lambda b, c: (layer, 0, 0)),
        pl.BlockSpec(memory_space=pl.ANY),
    ]
    return pl.pallas_call(
        functools.partial(_gla_kernel, zero_init=zero_init, sub=min(GLA_SUB, chunk)),
        grid=(bsz // nb, n_c),
        in_specs=in_specs,
        out_specs=[
            pl.BlockSpec((nb, chunk, GLA_WIDTH), lambda b, c: (b, c, 0)),
            _stack_slice_spec(layer, (nb, GLA_HEADS, GLA_HEAD_K, GLA_HEAD_V)),
        ],
        out_shape=[
            jax.ShapeDtypeStruct((bsz, t, GLA_WIDTH), BF16),
            jax.ShapeDtypeStruct(stack.shape, F32),
        ],
        input_output_aliases={len(in_specs) - 1: 1},
        compiler_params=pltpu.CompilerParams(
            dimension_semantics=("parallel", "arbitrary"), vmem_limit_bytes=VMEM_LIMIT_BYTES),
        name="gla",
    )(proj, s0_arg, up_p, gk_b, norm_g, stack)


def _bdot(a, b):
    return lax.dot_general(a.astype(BF16), b.astype(BF16), (((2,), (1,)), ((0,), (0,))),
                           preferred_element_type=F32)


def _bdot_nt(a, b):
    return lax.dot_general(a.astype(BF16), b.astype(BF16), (((2,), (2,)), ((0,), (0,))),
                           preferred_element_type=F32)


def _bdot_tn(a, b):
    return lax.dot_general(a.astype(BF16), b.astype(BF16), (((1,), (1,)), ((0,), (0,))),
                           preferred_element_type=F32)


def _pair_tiles(s_heads):
    return jnp.where(_pair_mask(), jnp.concatenate([s_heads, s_heads], axis=-1), 0.0)


def _pair_heads(tiles):
    first = _iota((LANES, RWKV_HEAD), 0) < RWKV_HEAD
    return jnp.where(first, tiles[:, :, :RWKV_HEAD], tiles[:, :, RWKV_HEAD:])


def _pair_mask():
    return (_iota((LANES, LANES), 0) < RWKV_HEAD) == (_iota((LANES, LANES), 1) < RWKV_HEAD)


def _rwkv_kernel(pr_ref, sh0_ref, s0_ref, mu_ref, w0_ref, w2_ref, a0_ref, a2_ref, g2_ref, kk_ref,
                 ka_ref, rk_ref, lnw_ref, lnb_ref, stack_ref, o_ref, s_out_ref, sh_ref, s_ref, *, zero_init):
    del stack_ref
    c_idx = pl.program_id(1)
    nb, c, _ = pr_ref.shape
    rows = nb * c
    log_c = c.bit_length() - 1
    r2 = 2 * c
    g_all = nb * N_PAIR

    @pl.when(c_idx == 0)
    def _():
        sh_ref[...] = sh0_ref[...]
        if zero_init:
            s_ref[...] = jnp.zeros_like(s_ref)
        else:
            s_ref[...] = _pair_tiles(s0_ref[...].reshape(g_all, LANES, RWKV_HEAD)).reshape(s_ref.shape)

    pr = pltpu.roll(pr_ref[...].reshape(rows, SEG), SEG - PR_LEAD, 1)
    prev_rows = jnp.broadcast_to(sh_ref[...], (nb, c, SEG)).reshape(rows, SEG)
    first_tok = (_iota((rows, 1), 0) & (c - 1)) == 0
    pr_prev = jnp.where(first_tok, prev_rows, pltpu.roll(pr, 1, 0))
    sh_ref[...] = pr.reshape(nb, c, SEG)[:, c - 1:c, :]
    xr = pr + (pr_prev - pr) * mu_ref[...]

    r = xr[:, PR_R0:PR_R0 + RWKV_WIDTH]
    kr = xr[:, PR_K0:PR_K0 + RWKV_WIDTH]
    vr = xr[:, PR_V0:PR_V0 + RWKV_WIDTH]
    x_wa = xr[:, PR_WA0:PR_WA0 + LANES]
    x_g = xr[:, PR_G0:PR_G0 + PR_GW]
    w_inner = -_softplus(-(w0_ref[...] + _dot(jnp.tanh(x_wa), w2_ref[...]))) - 0.5
    log_w = -jnp.exp(w_inner)
    a = jax.nn.sigmoid(a0_ref[...] + _dot(x_wa, a2_ref[...]))
    gate = _dot(jax.nn.sigmoid(x_g), g2_ref[...])
    k_mod = kr * (1.0 + (a - 1.0) * ka_ref[...])
    kk_raw = kr * kk_ref[...]

    ones_blk = ((_iota((LANES, LANES), 0) < RWKV_HEAD) == (_iota((LANES, LANES), 1) < RWKV_HEAD)).astype(BF16)

    def head_sums(x):
        return jnp.concatenate([_segsum(x[:, j * LANES:(j + 1) * LANES], ones_blk)
                                for j in range(RWKV_WIDTH // LANES)], axis=1)

    kk = kk_raw / jnp.maximum(jnp.sqrt(head_sums(kk_raw * kk_raw)), 1e-12)
    beta = kk * a

    seq_r = lax.shift_right_logical(_iota((rows, rows), 0), log_c)
    seq_c = lax.shift_right_logical(_iota((rows, rows), 1), log_c)
    same_seq = seq_r == seq_c
    tril = jnp.where(jnp.logical_and(same_seq, _iota((rows, rows), 1) <= _iota((rows, rows), 0)), 1.0, 0.0)
    cum = _dot01(tril, log_w)
    cum_ex = cum - log_w
    cum_last = jnp.concatenate(
        [jnp.broadcast_to(cum[(s + 1) * c - 1:(s + 1) * c], (c, RWKV_WIDTH)) for s in range(nb)], axis=0)
    e_neg = jnp.exp(-cum)
    e_last = jnp.exp(cum_last - cum)

    head_mask = (_iota((r2, LANES), 0) < c) == (_iota((r2, LANES), 1) < RWKV_HEAD)

    def groups(m):
        parts = []
        for s in range(nb):
            for p in range(N_PAIR):
                blk = m[s * c:(s + 1) * c, p * LANES:(p + 1) * LANES]
                parts.append(jnp.where(head_mask, jnp.concatenate([blk, blk], axis=0), 0.0))
        return jnp.stack(parts, axis=0)

    al_s = groups(-kk * jnp.exp(cum_ex))
    rt_s = groups(r * jnp.exp(cum))
    bh_s = groups(beta * e_neg)
    kh_s = groups(k_mod * e_neg)
    v_s = groups(vr)
    bb_s = groups(beta * e_last)
    kb_s = groups(k_mod * e_last)

    row2 = _iota((r2, r2), 0)
    col2 = _iota((r2, r2), 1)
    same_head = (row2 < c) == (col2 < c)
    lower_strict = jnp.logical_and(same_head, col2 < row2)
    lower_incl = jnp.logical_and(same_head, col2 <= row2)
    eye2 = jnp.where(row2 == col2, 1.0, 0.0)

    if r2 % LANES == 0:
        quad = _bdot_nt(jnp.concatenate([al_s, rt_s], axis=1), jnp.concatenate([bh_s, kh_s], axis=1))
        a_b = jnp.where(lower_strict, quad[:, :r2, :r2], 0.0)
        a_k = jnp.where(lower_strict, quad[:, :r2, r2:], 0.0)
        b_bk = jnp.where(jnp.concatenate([lower_incl, lower_incl], axis=1), quad[:, r2:, :], 0.0)
        out_intra = lambda u: _bdot(b_bk, jnp.concatenate([u, v_s], axis=1))
    else:
        a_b = jnp.where(lower_strict, _bdot_nt(al_s, bh_s), 0.0)
        a_k = jnp.where(lower_strict, _bdot_nt(al_s, kh_s), 0.0)
        b_b = jnp.where(lower_incl, _bdot_nt(rt_s, bh_s), 0.0)
        b_k = jnp.where(lower_incl, _bdot_nt(rt_s, kh_s), 0.0)
        out_intra = lambda u: _bdot(b_b, u) + _bdot(b_k, v_s)
    t_inv = eye2 + a_b
    pw = a_b
    for _ in range(max(log_c - 1, 0)):
        pw = _bdot(pw, pw)
        t_inv = t_inv + _bdot(t_inv, pw)

    s0 = s_ref[...].reshape(g_all, LANES, LANES)
    u_s = _bdot(t_inv, _bdot(a_k, v_s) + _bdot_nt(al_s, s0))
    o_s = _bdot_nt(rt_s, s0) + out_intra(u_s)

    decay = jnp.stack([jnp.exp(cum_last[s * c:s * c + 1, p * LANES:(p + 1) * LANES])
                       for s in range(nb) for p in range(N_PAIR)], axis=0)
    s_new = s0 * decay + _bdot_tn(jnp.concatenate([u_s, v_s], axis=1), jnp.concatenate([bb_s, kb_s], axis=1))
    s_ref[...] = s_new.reshape(s_ref.shape)

    @pl.when(c_idx == pl.num_programs(1) - 1)
    def _():
        s_out_ref[...] = _pair_heads(s_new).reshape(s_out_ref.shape)

    o_g = o_s[:, :c, :] + o_s[:, c:, :]
    o2 = jnp.concatenate([jnp.concatenate([o_g[s * N_PAIR + p] for p in range(N_PAIR)], axis=1)
                          for s in range(nb)], axis=0)

    mean = head_sums(o2) * (1.0 / RWKV_HEAD)
    cen = o2 - mean
    var = head_sums(cen * cen) * (1.0 / RWKV_HEAD)
    y = cen * lax.rsqrt(var + RWKV_GN_EPS) * lnw_ref[...] + lnb_ref[...]
    y = y + head_sums(r * k_mod * rk_ref[...]) * vr
    o_ref[...] = (y * gate).reshape(nb, c, RWKV_WIDTH).astype(o_ref.dtype)


def _rwkv(proj, shift0, s0, wts, layer, chunk, zero_init, stack):
    bsz, t, _ = proj.shape
    n_c = t // chunk
    nb = min(bsz, LANES // chunk, RWKV_SEQ_PER_BLOCK)
    assert bsz % nb == 0 and LANES % chunk == 0
    head_blk = (nb, RWKV_HEADS, RWKV_HEAD, RWKV_HEAD)
    if zero_init:
        s0_arg = jnp.zeros(head_blk, F32)
        s0_spec = pl.BlockSpec(head_blk, lambda b, c: (0, 0, 0, 0))
    else:
        s0_arg = s0
        s0_spec = pl.BlockSpec((None,) + head_blk, lambda b, c: (layer, b, 0, 0, 0))
    vec = lambda n: pl.BlockSpec((None, 1, n), lambda b, c: (layer, 0, 0))
    mat = lambda m, n: pl.BlockSpec((None, m, n), lambda b, c: (layer, 0, 0))
    in_specs = [
        pl.BlockSpec((nb, chunk, SEG), lambda b, c: (b, c, 0)),
        pl.BlockSpec((None, nb, 1, SEG), lambda b, c: (layer, b, 0, 0)),
        s0_spec,
        vec(SEG), vec(RWKV_WIDTH), mat(LANES, RWKV_WIDTH), vec(RWKV_WIDTH),
        mat(LANES, RWKV_WIDTH), mat(PR_GW, RWKV_WIDTH), vec(RWKV_WIDTH), vec(RWKV_WIDTH),
        vec(RWKV_WIDTH), vec(RWKV_WIDTH), vec(RWKV_WIDTH),
        pl.BlockSpec(memory_space=pl.ANY),
    ]
    return pl.pallas_call(
        functools.partial(_rwkv_kernel, zero_init=zero_init),
        grid=(bsz // nb, n_c),
        in_specs=in_specs,
        out_specs=[
            pl.BlockSpec((nb, chunk, RWKV_WIDTH), lambda b, c: (b, c, 0)),
            _stack_slice_spec(layer, head_blk),
            pl.BlockSpec((nb, 1, SEG), lambda b, c: (b, 0, 0)),
        ],
        out_shape=[
            jax.ShapeDtypeStruct((bsz, t, RWKV_WIDTH), BF16),
            jax.ShapeDtypeStruct(stack.shape, F32),
            jax.ShapeDtypeStruct((bsz, 1, SEG), F32),
        ],
        input_output_aliases={len(in_specs) - 1: 1},
        scratch_shapes=[pltpu.VMEM((nb, N_PAIR, LANES, LANES), F32)],
        compiler_params=pltpu.CompilerParams(
            dimension_semantics=("parallel", "arbitrary"), vmem_limit_bytes=VMEM_LIMIT_BYTES),
        name="rwkv",
    )(proj, shift0, s0_arg, *wts, stack)


def _pad_rows(w, rows_before, rows_total):
    depth, r, n = w.shape
    return jnp.concatenate([jnp.zeros((depth, rows_before, n), w.dtype), w,
                            jnp.zeros((depth, rows_total - rows_before - r, n), w.dtype)], axis=1)


def _trunk(x, mods, state_gla, state_wkv, state_shift_p, params, zero_init):
    (g_ffn1, w_ffn1_in, w_ffn1_out, g_mix, w_in_t, w_tail, up_p, gk_b, norm_g, rwkv_wts, w_out, g_ffn2,
     w_ffn2_in, w_ffn2_out, g_final) = params
    depth = w_in_t.shape[0]
    bsz, t, _ = x.shape
    gla_chunk = min(GLA_CHUNK, t)
    rwkv_chunk = min(RWKV_CHUNK, t)
    new_gla = jnp.zeros((depth, bsz, GLA_HEADS, GLA_HEAD_K, GLA_HEAD_V), F32)
    new_wkv = jnp.zeros((depth, bsz, RWKV_HEADS, RWKV_HEAD, RWKV_HEAD), F32)
    new_shift = []
    for layer in range(depth):
        x = _ffn(x, mods, layer, 0, g_ffn1, w_ffn1_in, w_ffn1_out, g_final, False)
        proj_g, proj_r = _inproj(x, mods, layer, g_mix, w_in_t, w_tail)
        o_gla, new_gla = _gla(proj_g, state_gla, up_p, gk_b, norm_g, layer, gla_chunk, zero_init, new_gla)
        o_w, new_wkv, shift = _rwkv(proj_r, state_shift_p, state_wkv, rwkv_wts, layer, rwkv_chunk,
                                    zero_init, new_wkv)
        x = _outproj(x, mods, layer, o_gla, o_w, w_out)
        x = _ffn(x, mods, layer, 6, g_ffn2, w_ffn2_in, w_ffn2_out, g_final, layer == depth - 1)
        new_shift.append(shift[:, 0, :RWKV_IN])
    return x, new_gla, new_wkv, jnp.stack(new_shift)


def kernel(x_prompt, x_sample, c_prompt, c_sample, state_gla, state_wkv, state_shift, w_ada, b_ada, g_ffn1, w_ffn1_in, w_ffn1_out, g_mix, w_in, gla_gk_up, gla_gk_b, gla_norm_g, rwkv_mu, rwkv_w0, rwkv_w2, rwkv_a0, rwkv_a2, rwkv_g2, rwkv_k_k, rwkv_k_a, rwkv_r_k, rwkv_ln_w, rwkv_ln_b, w_out, g_ffn2, w_ffn2_in, w_ffn2_out, g_final):
    depth = w_in.shape[0]
    bp, bs = x_prompt.shape[0], x_sample.shape[0]
    assert x_prompt.shape[-1] == D_MODEL and w_in.shape[-1] == GLA_IN + RWKV_IN

    n_seq = bp + bs
    n_seq_p = -(-n_seq // SUBLANES) * SUBLANES
    c_all = jnp.concatenate([c_sample, c_prompt, jnp.zeros((n_seq_p - n_seq, D_MODEL), F32)], axis=0)
    mods = _adaln(c_all, w_ada, b_ada)
    mods_p = mods[:, :, bs:n_seq, None, :]

    w_in_t = jnp.swapaxes(w_in, 1, 2)
    tail0 = (N_PROJ_TILES - 1) * PROJ_TILE
    w_tail = jnp.pad(w_in_t[:, tail0:, :], ((0, 0), (0, N_PROJ_TILES * PROJ_TILE - w_in_t.shape[1]), (0, 0)))
    row = lambda v: v.reshape(depth, 1, -1)
    pad_seg = lambda v: jnp.pad(v, [(0, 0)] * (v.ndim - 1) + [(0, SEG - RWKV_IN)])
    up_p = _pad_rows(gla_gk_up, 0, LANES)
    rwkv_wts = (
        row(pad_seg(rwkv_mu)), row(rwkv_w0), _pad_rows(rwkv_w2, 0, LANES), row(rwkv_a0),
        _pad_rows(rwkv_a2, RWKV_W_LORA, LANES), _pad_rows(rwkv_g2, 0, PR_GW), row(rwkv_k_k),
        row(rwkv_k_a), row(rwkv_r_k), row(rwkv_ln_w), row(rwkv_ln_b))
    params = (row(g_ffn1), w_ffn1_in, w_ffn1_out, row(g_mix), w_in_t, w_tail, up_p, row(gla_gk_b),
              row(gla_norm_g), rwkv_wts, w_out, row(g_ffn2), w_ffn2_in, w_ffn2_out,
              g_final.reshape(1, D_MODEL))

    shift_zero = jnp.zeros((depth, bp, 1, SEG), F32)
    y_p, gla_p, wkv_p, shift_p = _trunk(x_prompt, mods_p, None, None, shift_zero, params, True)
    shift_s0 = pad_seg(state_shift)[:, :, None, :]
    y_s, gla_s, wkv_s, shift_s = _trunk(x_sample, mods, state_gla, state_wkv, shift_s0, params, False)
    return (y_p, y_s, gla_p, wkv_p, shift_p, gla_s, wkv_s, shift_s)
```

```python
import functools

import jax
import jax.numpy as jnp
from jax import lax
from jax.experimental import pallas as pl
from jax.experimental.pallas import tpu as pltpu

F32 = jnp.float32
BF16 = jnp.bfloat16

D_MODEL = 2048
D_FF = 5632
N_MOD = 9
NORM_EPS = 1e-6
GLA_HEADS = 4
GLA_HEAD_K = 128
GLA_HEAD_V = 256
GLA_KEY = GLA_HEADS * GLA_HEAD_K
GLA_WIDTH = GLA_HEADS * GLA_HEAD_V
GLA_GATE_RANK = 16
GLA_GATE_NORMALIZER = 16.0
GLA_NORM_EPS = 1e-5
GLA_IN = 2 * GLA_KEY + 2 * GLA_WIDTH + GLA_GATE_RANK
RWKV_HEAD = 64
RWKV_HEADS = 16
RWKV_WIDTH = RWKV_HEADS * RWKV_HEAD
RWKV_W_LORA = 64
RWKV_A_LORA = 64
RWKV_G_LORA = 160
RWKV_IN = 3 * RWKV_WIDTH + RWKV_W_LORA + RWKV_A_LORA + RWKV_G_LORA
RWKV_GN_EPS = 64e-5

LANES = 128
SUBLANES = 8
VMEM_LIMIT_BYTES = 60 * 1024 * 1024

PROJ_TILE = 512
SEG = 7 * PROJ_TILE
N_PROJ_TILES = -(-(GLA_IN + RWKV_IN) // PROJ_TILE)
PR_WIN0 = (N_PROJ_TILES - 7) * PROJ_TILE
PR_LEAD = GLA_IN - PR_WIN0
PG_Q0, PG_K0, PG_V0 = 0, GLA_KEY, 2 * GLA_KEY
PG_A0 = PG_V0 + GLA_WIDTH
PR_R0, PR_K0, PR_V0, PR_WA0, PR_G0 = 0, 1024, 2048, 3072, 3200
PR_GW = 256
N_PAIR = RWKV_HEADS // 2

ROW_TILE = 1024
INPROJ_ROW_TILE = 1024
GLA_CHUNK = 64
GLA_SUB = 16
GLA_SEQ_PER_BLOCK = 8
RWKV_CHUNK = 64
RWKV_SEQ_PER_BLOCK = 8


def _dot(a, b):
    return jnp.dot(a.astype(BF16), b.astype(BF16), preferred_element_type=F32)


def _dot_nt(a, b):
    return lax.dot_general(a.astype(BF16), b.astype(BF16), (((1,), (1,)), ((), ())),
                           preferred_element_type=F32)


def _dot_tn(a, b):
    return lax.dot_general(a.astype(BF16), b.astype(BF16), (((0,), (0,)), ((), ())),
                           preferred_element_type=F32)


def _split3(x):
    hi = x.astype(BF16)
    r1 = x - hi.astype(F32)
    mid = r1.astype(BF16)
    lo = (r1 - mid.astype(F32)).astype(BF16)
    return hi, mid, lo


def _dot01(m01, x):
    m = m01.astype(BF16)
    hi, mid, lo = _split3(x)
    f = lambda t: jnp.dot(m, t, preferred_element_type=F32)
    return f(hi) + f(mid) + f(lo)


def _segsum(x, ones_blk):
    hi = x.astype(BF16)
    lo = (x - hi.astype(F32)).astype(BF16)
    return (jnp.dot(hi, ones_blk, preferred_element_type=F32)
            + jnp.dot(lo, ones_blk, preferred_element_type=F32))


def _iota(shape, dim):
    return lax.broadcasted_iota(jnp.int32, shape, dim)


def _softplus(z):
    return jnp.maximum(z, 0.0) + jnp.log(1.0 + jnp.exp(-jnp.abs(z)))


def _modulated_norm(x, g, shift, scale):
    y = x * lax.rsqrt(jnp.mean(x * x, axis=-1, keepdims=True) + NORM_EPS) * g
    return y * (1.0 + scale) + shift


def _adaln_kernel(c_ref, w_ref, b_ref, o_ref):
    c = c_ref[...]
    o_ref[...] = _dot(c * jax.nn.sigmoid(c), w_ref[...]) + b_ref[...]


def _adaln(c_all, w_ada, b_ada):
    depth = w_ada.shape[0]
    rows = c_all.shape[0]
    tn = 1024
    nn = D_MODEL // tn
    return pl.pallas_call(
        _adaln_kernel,
        grid=(depth, N_MOD, nn),
        in_specs=[
            pl.BlockSpec((rows, D_MODEL), lambda l, m, n: (0, 0)),
            pl.BlockSpec((None, D_MODEL, tn), lambda l, m, n: (l, 0, m * nn + n)),
            pl.BlockSpec((None, 1, tn), lambda l, m, n: (l, 0, m * nn + n)),
        ],
        out_specs=pl.BlockSpec((None, None, rows, tn), lambda l, m, n: (l, m, 0, n)),
        out_shape=jax.ShapeDtypeStruct((depth, N_MOD, rows, D_MODEL), F32),
        compiler_params=pltpu.CompilerParams(
            dimension_semantics=("parallel", "parallel", "parallel"),
            vmem_limit_bytes=VMEM_LIMIT_BYTES),
        name="adaln",
    )(c_all, w_ada, b_ada.reshape(depth, 1, N_MOD * D_MODEL))


def _row_tiling(bsz, t, row_tile=ROW_TILE):
    tb = min(t, row_tile)
    assert t % tb == 0 and tb % SUBLANES == 0
    sb = 1 if tb > SUBLANES else min(bsz, ROW_TILE // tb)
    assert bsz % sb == 0
    return sb, tb, t // tb


def _mod_spec(sb, n_t, layer, m, width=D_MODEL, col=lambda j: 0):
    if sb == 1:
        return pl.BlockSpec((None, None, 1, 1, width), lambda i, j: (layer, m, i // n_t, 0, col(j)))
    return pl.BlockSpec((None, None, sb, width), lambda i, j: (layer, m, i, col(j)))


SLAB = 256


def _slabs(sb, tb):
    n = min(SLAB, sb * tb)
    assert (sb * tb) % n == 0 and (sb == 1 or n % tb == 0)
    return [(r0, n) for r0 in range(0, sb * tb, n)]


def _tok_rows(x_ref, sb, tb, r0, n):
    if sb == 1:
        return x_ref[0, r0:r0 + n, :]
    return x_ref[r0 // tb:(r0 + n) // tb].reshape(n, x_ref.shape[-1])


def _put_rows(o_ref, sb, tb, r0, n, val):
    if sb == 1:
        o_ref[0, r0:r0 + n, :] = val
    else:
        o_ref[r0 // tb:(r0 + n) // tb] = val.reshape(n // tb, tb, val.shape[-1])


def _seq_rows(m_ref, sb, tb, r0, n):
    if sb == 1:
        return m_ref[0]
    n_seq = n // tb
    seq, tok = _iota((n, n_seq), 1), _iota((n, n_seq), 0)
    expand = jnp.where(jnp.logical_and(tok >= seq * tb, tok < (seq + 1) * tb), 1.0, 0.0)
    return _dot01(expand, m_ref[r0 // tb:(r0 + n) // tb])


def _write_modulated_norm(x_ref, g_ref, shift_ref, scale_ref, h_ref):
    sb, tb, _ = x_ref.shape
    for r0, n in _slabs(sb, tb):
        h = _modulated_norm(_tok_rows(x_ref, sb, tb, r0, n), g_ref[...],
                            _seq_rows(shift_ref, sb, tb, r0, n), _seq_rows(scale_ref, sb, tb, r0, n))
        h_ref[r0:r0 + n, :] = h.astype(BF16)


def _ffn_kernel(x_ref, shift_ref, scale_ref, gate_ref, g_ref, w1a_ref, w1b_ref, w2_ref, gfin_ref,
                o_ref, h_ref, *, final_norm):
    j = pl.program_id(1)
    sb, tb, d = x_ref.shape

    @pl.when(j == 0)
    def _():
        _write_modulated_norm(x_ref, g_ref, shift_ref, scale_ref, h_ref)
        o_ref[...] = jnp.zeros_like(o_ref)

    h = h_ref[...]
    u1 = jnp.dot(h, w1a_ref[...].astype(BF16), preferred_element_type=F32)
    u2 = jnp.dot(h, w1b_ref[...].astype(BF16), preferred_element_type=F32)
    act = (u1 * jax.nn.sigmoid(u1)) * u2
    o_ref[...] += _dot(act, w2_ref[...]).reshape(sb, tb, d)

    @pl.when(j == pl.num_programs(1) - 1)
    def _():
        for r0, n in _slabs(sb, tb):
            y = (_tok_rows(x_ref, sb, tb, r0, n)
                 + 0.5 * _seq_rows(gate_ref, sb, tb, r0, n) * _tok_rows(o_ref, sb, tb, r0, n))
            if final_norm:
                y = y * lax.rsqrt(jnp.mean(y * y, axis=-1, keepdims=True) + NORM_EPS) * gfin_ref[...]
            _put_rows(o_ref, sb, tb, r0, n, y)


def _ffn(x, mods, layer, m0, g, w_in, w_out, g_final, final_norm):
    bsz, t, d = x.shape
    sb, tb, n_t = _row_tiling(bsz, t)
    tf = 256
    nf = D_FF // tf
    x_spec = pl.BlockSpec((sb, tb, d), lambda i, j: (i // n_t, i % n_t, 0))
    return pl.pallas_call(
        functools.partial(_ffn_kernel, final_norm=final_norm),
        grid=((bsz // sb) * n_t, nf),
        in_specs=[
            x_spec,
            _mod_spec(sb, n_t, layer, m0), _mod_spec(sb, n_t, layer, m0 + 1),
            _mod_spec(sb, n_t, layer, m0 + 2),
            pl.BlockSpec((None, 1, d), lambda i, j: (layer, 0, 0)),
            pl.BlockSpec((None, d, tf), lambda i, j: (layer, 0, j)),
            pl.BlockSpec((None, d, tf), lambda i, j: (layer, 0, j + nf)),
            pl.BlockSpec((None, tf, d), lambda i, j: (layer, j, 0)),
            pl.BlockSpec((1, d), lambda i, j: (0, 0)),
        ],
        out_specs=x_spec,
        out_shape=jax.ShapeDtypeStruct(x.shape, F32),
        scratch_shapes=[pltpu.VMEM((sb * tb, d), BF16)],
        compiler_params=pltpu.CompilerParams(
            dimension_semantics=("parallel", "arbitrary"), vmem_limit_bytes=VMEM_LIMIT_BYTES),
        name="ffn",
    )(x, mods, mods, mods, g, w_in, w_in, w_out, g_final)


def _inproj_kernel(x_ref, shift_ref, scale_ref, g_ref, wt_ref, wtail_ref, og_ref, or_ref, h_ref):
    j = pl.program_id(1)
    shared = SEG // PROJ_TILE - 1
    last = N_PROJ_TILES - 1

    @pl.when(j == 0)
    def _():
        _write_modulated_norm(x_ref, g_ref, shift_ref, scale_ref, h_ref)

    def tile(w_ref):
        return lax.dot_general(h_ref[...], w_ref[...].astype(BF16), (((1,), (1,)), ((), ())),
                               preferred_element_type=F32)

    @pl.when(j < shared)
    def _():
        og_ref[...] = tile(wt_ref).reshape(og_ref.shape)

    @pl.when(j == shared)
    def _():
        both = tile(wt_ref)
        og_ref[...] = both.reshape(og_ref.shape)
        or_ref[...] = both.reshape(or_ref.shape)

    @pl.when(jnp.logical_and(j > shared, j < last))
    def _():
        or_ref[...] = tile(wt_ref).reshape(or_ref.shape)

    @pl.when(j == last)
    def _():
        or_ref[...] = tile(wtail_ref).reshape(or_ref.shape)


def _inproj(x, mods, layer, g, w_in_t, w_tail):
    bsz, t, d = x.shape
    sb, tb, n_t = _row_tiling(bsz, t, INPROJ_ROW_TILE)
    shared = SEG // PROJ_TILE - 1
    last = N_PROJ_TILES - 1
    row_map = lambda i: (i // n_t, i % n_t)
    return pl.pallas_call(
        _inproj_kernel,
        grid=((bsz // sb) * n_t, N_PROJ_TILES),
        in_specs=[
            pl.BlockSpec((sb, tb, d), lambda i, j: row_map(i) + (0,),
                         pipeline_mode=pl.Buffered(1)),
            _mod_spec(sb, n_t, layer, 3), _mod_spec(sb, n_t, layer, 4),
            pl.BlockSpec((None, 1, d), lambda i, j: (layer, 0, 0)),
            pl.BlockSpec((None, PROJ_TILE, d), lambda i, j: (layer, jnp.minimum(j, last - 1), 0)),
            pl.BlockSpec((None, PROJ_TILE, d), lambda i, j: (layer, 0, 0), pipeline_mode=pl.Buffered(1)),
        ],
        out_specs=[
            pl.BlockSpec((sb, tb, PROJ_TILE), lambda i, j: row_map(i) + (jnp.minimum(j, shared),)),
            pl.BlockSpec((sb, tb, PROJ_TILE), lambda i, j: row_map(i) + (jnp.maximum(j - shared, 0),)),
        ],
        out_shape=[jax.ShapeDtypeStruct((bsz, t, SEG), F32), jax.ShapeDtypeStruct((bsz, t, SEG), F32)],
        scratch_shapes=[pltpu.VMEM((sb * tb, d), BF16)],
        compiler_params=pltpu.CompilerParams(
            dimension_semantics=("parallel", "arbitrary"), vmem_limit_bytes=VMEM_LIMIT_BYTES),
        name="inproj",
    )(x, mods, mods, g, w_in_t, w_tail)


def _outproj_kernel(x_ref, gate_ref, og_ref, ow_ref, wg_ref, ww_ref, o_ref):
    sb, tb, tn = x_ref.shape
    og = og_ref[...].reshape(sb * tb, GLA_WIDTH)
    ow = ow_ref[...].reshape(sb * tb, RWKV_WIDTH)
    y = (jnp.dot(og, wg_ref[...].astype(BF16), preferred_element_type=F32)
         + jnp.dot(ow, ww_ref[...].astype(BF16), preferred_element_type=F32))
    o_ref[...] = (x_ref[...].reshape(sb * tb, tn)
                  + _seq_rows(gate_ref, sb, tb, 0, sb * tb) * y).reshape(sb, tb, tn)


def _outproj(x, mods, layer, o_gla, o_w, w_out):
    bsz, t, d = x.shape
    sb, tb, n_t = _row_tiling(bsz, t)
    tn = 1024
    x_spec = pl.BlockSpec((sb, tb, tn), lambda i, j: (i // n_t, i % n_t, j))
    o_spec = pl.BlockSpec((sb, tb, GLA_WIDTH), lambda i, j: (i // n_t, i % n_t, 0))
    return pl.pallas_call(
        _outproj_kernel,
        grid=((bsz // sb) * n_t, d // tn),
        in_specs=[
            x_spec,
            _mod_spec(sb, n_t, layer, 5, width=tn, col=lambda j: j),
            o_spec, o_spec,
            pl.BlockSpec((None, GLA_WIDTH, tn), lambda i, j: (layer, 0, j)),
            pl.BlockSpec((None, RWKV_WIDTH, tn), lambda i, j: (layer, 1, j)),
        ],
        out_specs=x_spec,
        out_shape=jax.ShapeDtypeStruct(x.shape, F32),
        compiler_params=pltpu.CompilerParams(
            dimension_semantics=("parallel", "arbitrary"), vmem_limit_bytes=VMEM_LIMIT_BYTES),
        name="outproj",
    )(x, mods, o_gla, o_w, w_out, w_out)


def _gla_kernel(pg_ref, s0_ref, up_ref, gb_ref, ng_ref, stack_ref, o_ref, s_ref, *, zero_init, sub):
    del stack_ref
    c_idx = pl.program_id(1)
    nb, c, _ = pg_ref.shape
    rows = nb * c
    log_c = c.bit_length() - 1
    n_sub = c // sub

    @pl.when(c_idx == 0)
    def _():
        if zero_init:
            s_ref[...] = jnp.zeros_like(s_ref)
        else:
            s_ref[...] = s0_ref[...]

    x_a = pg_ref[:, :, PG_A0:PG_A0 + LANES].reshape(rows, LANES)
    z = _dot(x_a, up_ref[...]) + gb_ref[...]
    gate_w = GLA_WIDTH + LANES
    gates = pltpu.roll(pg_ref[:, :, PG_A0:PG_A0 + gate_w].reshape(rows, gate_w),
                       gate_w - GLA_GATE_RANK, 1)
    glog = -_softplus(-z) / GLA_GATE_NORMALIZER
    r_i, c_i = _iota((rows, rows), 0), _iota((rows, rows), 1)
    same_seq = lax.shift_right_logical(r_i, log_c) == lax.shift_right_logical(c_i, log_c)
    tril = jnp.where(jnp.logical_and(same_seq, c_i <= r_i), 1.0, 0.0)
    b_all = _dot01(tril, glog)
    causal_ss = _iota((sub, sub), 1) <= _iota((sub, sub), 0)
    lane_ss = _iota((sub, sub), 1)

    grp = [(s, h) for s in range(nb) for h in range(GLA_HEADS)]
    kcols = lambda base, h: slice(base + h * GLA_HEAD_K, base + (h + 1) * GLA_HEAD_K)
    vcols = lambda base, h: slice(base + h * GLA_HEAD_V, base + (h + 1) * GLA_HEAD_V)
    q = jnp.stack([pg_ref[s, :, kcols(PG_Q0, h)] for s, h in grp]) * (GLA_HEAD_K ** -0.5)
    k = jnp.stack([pg_ref[s, :, kcols(PG_K0, h)] for s, h in grp])
    v = jnp.stack([pg_ref[s, :, vcols(PG_V0, h)] for s, h in grp])
    gate = jnp.stack([gates[s * c:(s + 1) * c, vcols(0, h)] for s, h in grp])
    b = jnp.stack([b_all[s * c:(s + 1) * c, kcols(0, h)] for s, h in grp])
    s0 = s_ref[...].reshape(len(grp), GLA_HEAD_K, GLA_HEAD_V)

    o_inter = _bdot(q * jnp.exp(b), s0)
    o_parts = []
    for i_sub in range(n_sub):
        r0 = i_sub * sub
        q_i, b_i = q[:, r0:r0 + sub], b[:, r0:r0 + sub]
        diag = jnp.zeros((len(grp), sub, sub), F32)
        for jj in range(sub):
            dec = jnp.exp(jnp.minimum(b_i - b[:, r0 + jj:r0 + jj + 1], 0.0))
            col = jnp.sum(q_i * dec * k[:, r0 + jj:r0 + jj + 1], axis=-1, keepdims=True)
            diag = jnp.where(lane_ss == jj, col, diag)
        o_i = _bdot(jnp.where(causal_ss, diag, 0.0), v[:, r0:r0 + sub])
        if i_sub > 0:
            b_start = b[:, r0 - 1:r0]
            q_rel = q_i * jnp.exp(b_i - b_start)
            k_rel = k[:, :r0] * jnp.exp(b_start - b[:, :r0])
            o_i = o_i + _bdot(_bdot_nt(q_rel, k_rel), v[:, :r0])
        o_parts.append(o_i)
    o = o_inter + (jnp.concatenate(o_parts, axis=1) if n_sub > 1 else o_parts[0])

    o = o * lax.rsqrt(jnp.mean(o * o, axis=-1, keepdims=True) + GLA_NORM_EPS) * ng_ref[...]
    o = (o * (gate * jax.nn.sigmoid(gate))).astype(o_ref.dtype)
    for g, (s, h) in enumerate(grp):
        o_ref[s, :, vcols(0, h)] = o[g]

    b_last = b[:, c - 1:c]
    k_bar = k * jnp.exp(b_last - b)
    decay_col = jnp.stack([
        jnp.concatenate([jnp.exp(jnp.broadcast_to(b_last[g], (GLA_HEAD_K, GLA_HEAD_K)).T)]
                        * (GLA_HEAD_V // GLA_HEAD_K), axis=1) for g in range(len(grp))])
    s_ref[...] = (s0 * decay_col + _bdot_tn(k_bar, v)).reshape(s_ref.shape)


def _stack_slice_spec(layer, blk):
    return pl.BlockSpec((None,) + blk, lambda b, c: (layer, b) + (0,) * (len(blk) - 1))


def _gla(proj, s0, up_p, gk_b, norm_g, layer, chunk, zero_init, stack):
    bsz, t, _ = proj.shape
    n_c = t // chunk
    nb = min(bsz, LANES // chunk, GLA_SEQ_PER_BLOCK)
    assert bsz % nb == 0
    if zero_init:
        s0_arg = jnp.zeros((nb, GLA_HEADS, GLA_HEAD_K, GLA_HEAD_V), F32)
        s0_spec = pl.BlockSpec((nb, GLA_HEADS, GLA_HEAD_K, GLA_HEAD_V), lambda b, c: (0, 0, 0, 0))
    else:
        s0_arg = s0
        s0_spec = pl.BlockSpec((None, nb, GLA_HEADS, GLA_HEAD_K, GLA_HEAD_V),
                               lambda b, c: (layer, b, 0, 0, 0))
    in_specs = [
        pl.BlockSpec((nb, chunk, SEG), lambda b, c: (b, c, 0)),
        s0_spec,
        pl.BlockSpec((None, LANES, GLA_KEY), lambda b, c: (layer, 0, 0)),
        pl.BlockSpec((None, 1, GLA_KEY), lambda b, c: (layer, 0, 0)),
        pl.BlockSpec((None, 1, GLA_HEAD_V), lambda b, c: (layer, 0, 0)),
        pl.BlockSpec(memory_space=pl.ANY),
    ]
    return pl.pallas_call(
        functools.partial(_gla_kernel, zero_init=zero_init, sub=min(GLA_SUB, chunk)),
        grid=(bsz // nb, n_c),
        in_specs=in_specs,
        out_specs=[
            pl.BlockSpec((nb, chunk, GLA_WIDTH), lambda b, c: (b, c, 0)),
            _stack_slice_spec(layer, (nb, GLA_HEADS, GLA_HEAD_K, GLA_HEAD_V)),
        ],
        out_shape=[
            jax.ShapeDtypeStruct((bsz, t, GLA_WIDTH), BF16),
            jax.ShapeDtypeStruct(stack.shape, F32),
        ],
        input_output_aliases={len(in_specs) - 1: 1},
        compiler_params=pltpu.CompilerParams(
            dimension_semantics=("parallel", "arbitrary"), vmem_limit_bytes=VMEM_LIMIT_BYTES),
        name="gla",
    )(proj, s0_arg, up_p, gk_b, norm_g, stack)


def _bdot(a, b):
    return lax.dot_general(a.astype(BF16), b.astype(BF16), (((2,), (1,)), ((0,), (0,))),
                           preferred_element_type=F32)


def _bdot_nt(a, b):
    return lax.dot_general(a.astype(BF16), b.astype(BF16), (((2,), (2,)), ((0,), (0,))),
                           preferred_element_type=F32)


def _bdot_tn(a, b):
    return lax.dot_general(a.astype(BF16), b.astype(BF16), (((1,), (1,)), ((0,), (0,))),
                           preferred_element_type=F32)


def _pair_tiles(s_heads):
    return jnp.where(_pair_mask(), jnp.concatenate([s_heads, s_heads], axis=-1), 0.0)


def _pair_heads(tiles):
    first = _iota((LANES, RWKV_HEAD), 0) < RWKV_HEAD
    return jnp.where(first, tiles[:, :, :RWKV_HEAD], tiles[:, :, RWKV_HEAD:])


def _pair_mask():
    return (_iota((LANES, LANES), 0) < RWKV_HEAD) == (_iota((LANES, LANES), 1) < RWKV_HEAD)


def _rwkv_kernel(pr_ref, sh0_ref, s0_ref, mu_ref, w0_ref, w2_ref, a0_ref, a2_ref, g2_ref, kk_ref,
                 ka_ref, rk_ref, lnw_ref, lnb_ref, stack_ref, o_ref, s_out_ref, sh_ref, s_ref, *, zero_init):
    del stack_ref
    c_idx = pl.program_id(1)
    nb, c, _ = pr_ref.shape
    rows = nb * c
    log_c = c.bit_length() - 1
    r2 = 2 * c
    g_all = nb * N_PAIR

    @pl.when(c_idx == 0)
    def _():
        sh_ref[...] = sh0_ref[...]
        if zero_init:
            s_ref[...] = jnp.zeros_like(s_ref)
        else:
            s_ref[...] = _pair_tiles(s0_ref[...].reshape(g_all, LANES, RWKV_HEAD)).reshape(s_ref.shape)

    pr = pltpu.roll(pr_ref[...].reshape(rows, SEG), SEG - PR_LEAD, 1)
    prev_rows = jnp.broadcast_to(sh_ref[...], (nb, c, SEG)).reshape(rows, SEG)
    first_tok = (_iota((rows, 1), 0) & (c - 1)) == 0
    pr_prev = jnp.where(first_tok, prev_rows, pltpu.roll(pr, 1, 0))
    sh_ref[...] = pr.reshape(nb, c, SEG)[:, c - 1:c, :]
    xr = pr + (pr_prev - pr) * mu_ref[...]

    r = xr[:, PR_R0:PR_R0 + RWKV_WIDTH]
    kr = xr[:, PR_K0:PR_K0 + RWKV_WIDTH]
    vr = xr[:, PR_V0:PR_V0 + RWKV_WIDTH]
    x_wa = xr[:, PR_WA0:PR_WA0 + LANES]
    x_g = xr[:, PR_G0:PR_G0 + PR_GW]
    w_inner = -_softplus(-(w0_ref[...] + _dot(jnp.tanh(x_wa), w2_ref[...]))) - 0.5
    log_w = -jnp.exp(w_inner)
    a = jax.nn.sigmoid(a0_ref[...] + _dot(x_wa, a2_ref[...]))
    gate = _dot(jax.nn.sigmoid(x_g), g2_ref[...])
    k_mod = kr * (1.0 + (a - 1.0) * ka_ref[...])
    kk_raw = kr * kk_ref[...]

    ones_blk = ((_iota((LANES, LANES), 0) < RWKV_HEAD) == (_iota((LANES, LANES), 1) < RWKV_HEAD)).astype(BF16)

    def head_sums(x):
        return jnp.concatenate([_segsum(x[:, j * LANES:(j + 1) * LANES], ones_blk)
                                for j in range(RWKV_WIDTH // LANES)], axis=1)

    kk = kk_raw / jnp.maximum(jnp.sqrt(head_sums(kk_raw * kk_raw)), 1e-12)
    beta = kk * a

    seq_r = lax.shift_right_logical(_iota((rows, rows), 0), log_c)
    seq_c = lax.shift_right_logical(_iota((rows, rows), 1), log_c)
    same_seq = seq_r == seq_c
    tril = jnp.where(jnp.logical_and(same_seq, _iota((rows, rows), 1) <= _iota((rows, rows), 0)), 1.0, 0.0)
    cum = _dot01(tril, log_w)
    cum_ex = cum - log_w
    cum_last = jnp.concatenate(
        [jnp.broadcast_to(cum[(s + 1) * c - 1:(s + 1) * c], (c, RWKV_WIDTH)) for s in range(nb)], axis=0)
    e_neg = jnp.exp(-cum)
    e_last = jnp.exp(cum_last - cum)

    head_mask = (_iota((r2, LANES), 0) < c) == (_iota((r2, LANES), 1) < RWKV_HEAD)

    def groups(m):
        parts = []
        for s in range(nb):
            for p in range(N_PAIR):
                blk = m[s * c:(s + 1) * c, p * LANES:(p + 1) * LANES]
                parts.append(jnp.where(head_mask, jnp.concatenate([blk, blk], axis=0), 0.0))
        return jnp.stack(parts, axis=0)

    al_s = groups(-kk * jnp.exp(cum_ex))
    rt_s = groups(r * jnp.exp(cum))
    bh_s = groups(beta * e_neg)
    kh_s = groups(k_mod * e_neg)
    v_s = groups(vr)
    bb_s = groups(beta * e_last)
    kb_s = groups(k_mod * e_last)

    row2 = _iota((r2, r2), 0)
    col2 = _iota((r2, r2), 1)
    same_head = (row2 < c) == (col2 < c)
    lower_strict = jnp.logical_and(same_head, col2 < row2)
    lower_incl = jnp.logical_and(same_head, col2 <= row2)
    eye2 = jnp.where(row2 == col2, 1.0, 0.0)

    if r2 % LANES == 0:
        quad = _bdot_nt(jnp.concatenate([al_s, rt_s], axis=1), jnp.concatenate([bh_s, kh_s], axis=1))
        a_b = jnp.where(lower_strict, quad[:, :r2, :r2], 0.0)
        a_k = jnp.where(lower_strict, quad[:, :r2, r2:], 0.0)
        b_bk = jnp.where(jnp.concatenate([lower_incl, lower_incl], axis=1), quad[:, r2:, :], 0.0)
        out_intra = lambda u: _bdot(b_bk, jnp.concatenate([u, v_s], axis=1))
    else:
        a_b = jnp.where(lower_strict, _bdot_nt(al_s, bh_s), 0.0)
        a_k = jnp.where(lower_strict, _bdot_nt(al_s, kh_s), 0.0)
        b_b = jnp.where(lower_incl, _bdot_nt(rt_s, bh_s), 0.0)
        b_k = jnp.where(lower_incl, _bdot_nt(rt_s, kh_s), 0.0)
        out_intra = lambda u: _bdot(b_b, u) + _bdot(b_k, v_s)
    t_inv = eye2 + a_b
    pw = a_b
    for _ in range(max(log_c - 1, 0)):
        pw = _bdot(pw, pw)
        t_inv = t_inv + _bdot(t_inv, pw)

    s0 = s_ref[...].reshape(g_all, LANES, LANES)
    u_s = _bdot(t_inv, _bdot(a_k, v_s) + _bdot_nt(al_s, s0))
    o_s = _bdot_nt(rt_s, s0) + out_intra(u_s)

    decay = jnp.stack([jnp.exp(cum_last[s * c:s * c + 1, p * LANES:(p + 1) * LANES])
                       for s in range(nb) for p in range(N_PAIR)], axis=0)
    s_new = s0 * decay + _bdot_tn(jnp.concatenate([u_s, v_s], axis=1), jnp.concatenate([bb_s, kb_s], axis=1))
    s_ref[...] = s_new.reshape(s_ref.shape)

    @pl.when(c_idx == pl.num_programs(1) - 1)
    def _():
        s_out_ref[...] = _pair_heads(s_new).reshape(s_out_ref.shape)

    o_g = o_s[:, :c, :] + o_s[:, c:, :]
    o2 = jnp.concatenate([jnp.concatenate([o_g[s * N_PAIR + p] for p in range(N_PAIR)], axis=1)
                          for s in range(nb)], axis=0)

    mean = head_sums(o2) * (1.0 / RWKV_HEAD)
    cen = o2 - mean
    var = head_sums(cen * cen) * (1.0 / RWKV_HEAD)
    y = cen * lax.rsqrt(var + RWKV_GN_EPS) * lnw_ref[...] + lnb_ref[...]
    y = y + head_sums(r * k_mod * rk_ref[...]) * vr
    o_ref[...] = (y * gate).reshape(nb, c, RWKV_WIDTH).astype(o_ref.dtype)


def _rwkv(proj, shift0, s0, wts, layer, chunk, zero_init, stack):
    bsz, t, _ = proj.shape
    n_c = t // chunk
    nb = min(bsz, LANES // chunk, RWKV_SEQ_PER_BLOCK)
    assert bsz % nb == 0 and LANES % chunk == 0
    head_blk = (nb, RWKV_HEADS, RWKV_HEAD, RWKV_HEAD)
    if zero_init:
        s0_arg = jnp.zeros(head_blk, F32)
        s0_spec = pl.BlockSpec(head_blk, lambda b, c: (0, 0, 0, 0))
    else:
        s0_arg = s0
        s0_spec = pl.BlockSpec((None,) + head_blk, lambda b, c: (layer, b, 0, 0, 0))
    vec = lambda n: pl.BlockSpec((None, 1, n), lambda b, c: (layer, 0, 0))
    mat = lambda m, n: pl.BlockSpec((None, m, n), lambda b, c: (layer, 0, 0))
    in_specs = [
        pl.BlockSpec((nb, chunk, SEG), lambda b, c: (b, c, 0)),
        pl.BlockSpec((None, nb, 1, SEG), lambda b, c: (layer, b, 0, 0)),
        s0_spec,
        vec(SEG), vec(RWKV_WIDTH), mat(LANES, RWKV_WIDTH), vec(RWKV_WIDTH),
        mat(LANES, RWKV_WIDTH), mat(PR_GW, RWKV_WIDTH), vec(RWKV_WIDTH), vec(RWKV_WIDTH),
        vec(RWKV_WIDTH), vec(RWKV_WIDTH), vec(RWKV_WIDTH),
        pl.BlockSpec(memory_space=pl.ANY),
    ]
    return pl.pallas_call(
        functools.partial(_rwkv_kernel, zero_init=zero_init),
        grid=(bsz // nb, n_c),
        in_specs=in_specs,
        out_specs=[
            pl.BlockSpec((nb, chunk, RWKV_WIDTH), lambda b, c: (b, c, 0)),
            _stack_slice_spec(layer, head_blk),
            pl.BlockSpec((nb, 1, SEG), lambda b, c: (b, 0, 0)),
        ],
        out_shape=[
            jax.ShapeDtypeStruct((bsz, t, RWKV_WIDTH), BF16),
            jax.ShapeDtypeStruct(stack.shape, F32),
            jax.ShapeDtypeStruct((bsz, 1, SEG), F32),
        ],
        input_output_aliases={len(in_specs) - 1: 1},
        scratch_shapes=[pltpu.VMEM((nb, N_PAIR, LANES, LANES), F32)],
        compiler_params=pltpu.CompilerParams(
            dimension_semantics=("parallel", "arbitrary"), vmem_limit_bytes=VMEM_LIMIT_BYTES),
        name="rwkv",
    )(proj, shift0, s0_arg, *wts, stack)


def _pad_rows(w, rows_before, rows_total):
    depth, r, n = w.shape
    return jnp.concatenate([jnp.zeros((depth, rows_before, n), w.dtype), w,
                            jnp.zeros((depth, rows_total - rows_before - r, n), w.dtype)], axis=1)


def _trunk(x, mods, state_gla, state_wkv, state_shift_p, params, zero_init):
    (g_ffn1, w_ffn1_in, w_ffn1_out, g_mix, w_in_t, w_tail, up_p, gk_b, norm_g, rwkv_wts, w_out, g_ffn2,
     w_ffn2_in, w_ffn2_out, g_final) = params
    depth = w_in_t.shape[0]
    bsz, t, _ = x.shape
    gla_chunk = min(GLA_CHUNK, t)
    rwkv_chunk = min(RWKV_CHUNK, t)
    new_gla = jnp.zeros((depth, bsz, GLA_HEADS, GLA_HEAD_K, GLA_HEAD_V), F32)
    new_wkv = jnp.zeros((depth, bsz, RWKV_HEADS, RWKV_HEAD, RWKV_HEAD), F32)
    new_shift = []
    for layer in range(depth):
        x = _ffn(x, mods, layer, 0, g_ffn1, w_ffn1_in, w_ffn1_out, g_final, False)
        proj_g, proj_r = _inproj(x, mods, layer, g_mix, w_in_t, w_tail)
        o_gla, new_gla = _gla(proj_g, state_gla, up_p, gk_b, norm_g, layer, gla_chunk, zero_init, new_gla)
        o_w, new_wkv, shift = _rwkv(proj_r, state_shift_p, state_wkv, rwkv_wts, layer, rwkv_chunk,
                                    zero_init, new_wkv)
        x = _outproj(x, mods, layer, o_gla, o_w, w_out)
        x = _ffn(x, mods, layer, 6, g_ffn2, w_ffn2_in, w_ffn2_out, g_final, layer == depth - 1)
        new_shift.append(shift[:, 0, :RWKV_IN])
    return x, new_gla, new_wkv, jnp.stack(new_shift)


def kernel(x_prompt, x_sample, c_prompt, c_sample, state_gla, state_wkv, state_shift, w_ada, b_ada, g_ffn1, w_ffn1_in, w_ffn1_out, g_mix, w_in, gla_gk_up, gla_gk_b, gla_norm_g, rwkv_mu, rwkv_w0, rwkv_w2, rwkv_a0, rwkv_a2, rwkv_g2, rwkv_k_k, rwkv_k_a, rwkv_r_k, rwkv_ln_w, rwkv_ln_b, w_out, g_ffn2, w_ffn2_in, w_ffn2_out, g_final):
    depth = w_in.shape[0]
    bp, bs = x_prompt.shape[0], x_sample.shape[0]
    assert x_prompt.shape[-1] == D_MODEL and w_in.shape[-1] == GLA_IN + RWKV_IN

    n_seq = bp + bs
    n_seq_p = -(-n_seq // SUBLANES) * SUBLANES
    c_all = jnp.concatenate([c_sample, c_prompt, jnp.zeros((n_seq_p - n_seq, D_MODEL), F32)], axis=0)
    mods = _adaln(c_all, w_ada, b_ada)
    mods_p = mods[:, :, bs:n_seq, None, :]

    w_in_t = jnp.swapaxes(w_in, 1, 2)
    tail0 = (N_PROJ_TILES - 1) * PROJ_TILE
    w_tail = jnp.pad(w_in_t[:, tail0:, :], ((0, 0), (0, N_PROJ_TILES * PROJ_TILE - w_in_t.shape[1]), (0, 0)))
    row = lambda v: v.reshape(depth, 1, -1)
    pad_seg = lambda v: jnp.pad(v, [(0, 0)] * (v.ndim - 1) + [(0, SEG - RWKV_IN)])
    up_p = _pad_rows(gla_gk_up, 0, LANES)
    rwkv_wts = (
        row(pad_seg(rwkv_mu)), row(rwkv_w0), _pad_rows(rwkv_w2, 0, LANES), row(rwkv_a0),
        _pad_rows(rwkv_a2, RWKV_W_LORA, LANES), _pad_rows(rwkv_g2, 0, PR_GW), row(rwkv_k_k),
        row(rwkv_k_a), row(rwkv_r_k), row(rwkv_ln_w), row(rwkv_ln_b))
    params = (row(g_ffn1), w_ffn1_in, w_ffn1_out, row(g_mix), w_in_t, w_tail, up_p, row(gla_gk_b),
              row(gla_norm_g), rwkv_wts, w_out, row(g_ffn2), w_ffn2_in, w_ffn2_out,
              g_final.reshape(1, D_MODEL))

    shift_zero = jnp.zeros((depth, bp, 1, SEG), F32)
    y_p, gla_p, wkv_p, shift_p = _trunk(x_prompt, mods_p, None, None, shift_zero, params, True)
    shift_s0 = pad_seg(state_shift)[:, :, None, :]
    y_s, gla_s, wkv_s, shift_s = _trunk(x_sample, mods, state_gla, state_wkv, shift_s0, params, False)
    return (y_p, y_s, gla_p, wkv_p, shift_p, gla_s, wkv_s, shift_s)
```

```python
import functools

import jax
import jax.numpy as jnp
from jax import lax
from jax.experimental import pallas as pl
from jax.experimental.pallas import tpu as pltpu

F32 = jnp.float32
BF16 = jnp.bfloat16

D_MODEL = 2048
D_FF = 5632
N_MOD = 9
NORM_EPS = 1e-6
GLA_HEADS = 4
GLA_HEAD_K = 128
GLA_HEAD_V = 256
GLA_KEY = GLA_HEADS * GLA_HEAD_K
GLA_WIDTH = GLA_HEADS * GLA_HEAD_V
GLA_GATE_RANK = 16
GLA_GATE_NORMALIZER = 16.0
GLA_NORM_EPS = 1e-5
LOG2_E = 1.4426950408889634
GLA_IN = 2 * GLA_KEY + 2 * GLA_WIDTH + GLA_GATE_RANK
RWKV_HEAD = 64
RWKV_HEADS = 16
RWKV_WIDTH = RWKV_HEADS * RWKV_HEAD
RWKV_W_LORA = 64
RWKV_A_LORA = 64
RWKV_G_LORA = 160
RWKV_IN = 3 * RWKV_WIDTH + RWKV_W_LORA + RWKV_A_LORA + RWKV_G_LORA
RWKV_GN_EPS = 64e-5

LANES = 128
SUBLANES = 8
VMEM_LIMIT_BYTES = 60 * 1024 * 1024

PROJ_TILE = 512
SEG = 7 * PROJ_TILE
N_PROJ_TILES = -(-(GLA_IN + RWKV_IN) // PROJ_TILE)
PR_WIN0 = (N_PROJ_TILES - 7) * PROJ_TILE
PR_LEAD = GLA_IN - PR_WIN0
PG_Q0, PG_K0, PG_V0 = 0, GLA_KEY, 2 * GLA_KEY
PG_A0 = PG_V0 + GLA_WIDTH
PR_R0, PR_K0, PR_V0, PR_WA0, PR_G0 = 0, 1024, 2048, 3072, 3200
PR_GW = 256
N_PAIR = RWKV_HEADS // 2

ROW_TILE = 1024
INPROJ_ROW_TILE = 1024
GLA_CHUNK = 64
GLA_SUB = 16
GLA_SEQ_PER_BLOCK = 8
RWKV_CHUNK = 64
RWKV_SEQ_PER_BLOCK = 8


def _dot(a, b):
    return jnp.dot(a.astype(BF16), b.astype(BF16), preferred_element_type=F32)


def _dot_nt(a, b):
    return lax.dot_general(a.astype(BF16), b.astype(BF16), (((1,), (1,)), ((), ())),
                           preferred_element_type=F32)


def _dot_tn(a, b):
    return lax.dot_general(a.astype(BF16), b.astype(BF16), (((0,), (0,)), ((), ())),
                           preferred_element_type=F32)


def _split3(x):
    hi = x.astype(BF16)
    r1 = x - hi.astype(F32)
    mid = r1.astype(BF16)
    lo = (r1 - mid.astype(F32)).astype(BF16)
    return hi, mid, lo


def _dot01(m01, x):
    m = m01.astype(BF16)
    hi, mid, lo = _split3(x)
    f = lambda t: jnp.dot(m, t, preferred_element_type=F32)
    return f(hi) + f(mid) + f(lo)


def _segsum(x, ones_blk):
    hi = x.astype(BF16)
    lo = (x - hi.astype(F32)).astype(BF16)
    return (jnp.dot(hi, ones_blk, preferred_element_type=F32)
            + jnp.dot(lo, ones_blk, preferred_element_type=F32))


def _iota(shape, dim):
    return lax.broadcasted_iota(jnp.int32, shape, dim)


def _softplus(z):
    return jnp.maximum(z, 0.0) + jnp.log(1.0 + jnp.exp(-jnp.abs(z)))


def _modulated_norm(x, g, shift, scale):
    y = x * lax.rsqrt(jnp.mean(x * x, axis=-1, keepdims=True) + NORM_EPS) * g
    return y * (1.0 + scale) + shift


def _adaln_kernel(c_ref, w_ref, b_ref, o_ref):
    c = c_ref[...]
    o_ref[...] = _dot(c * jax.nn.sigmoid(c), w_ref[...]) + b_ref[...]


def _adaln(c_all, w_ada, b_ada):
    depth = w_ada.shape[0]
    rows = c_all.shape[0]
    tn = 1024
    nn = D_MODEL // tn
    return pl.pallas_call(
        _adaln_kernel,
        grid=(depth, N_MOD, nn),
        in_specs=[
            pl.BlockSpec((rows, D_MODEL), lambda l, m, n: (0, 0)),
            pl.BlockSpec((None, D_MODEL, tn), lambda l, m, n: (l, 0, m * nn + n)),
            pl.BlockSpec((None, 1, tn), lambda l, m, n: (l, 0, m * nn + n)),
        ],
        out_specs=pl.BlockSpec((None, None, rows, tn), lambda l, m, n: (l, m, 0, n)),
        out_shape=jax.ShapeDtypeStruct((depth, N_MOD, rows, D_MODEL), F32),
        compiler_params=pltpu.CompilerParams(
            dimension_semantics=("parallel", "parallel", "parallel"),
            vmem_limit_bytes=VMEM_LIMIT_BYTES),
        name="adaln",
    )(c_all, w_ada, b_ada.reshape(depth, 1, N_MOD * D_MODEL))


def _row_tiling(bsz, t, row_tile=ROW_TILE):
    tb = min(t, row_tile)
    assert t % tb == 0 and tb % SUBLANES == 0
    sb = 1 if tb > SUBLANES else min(bsz, ROW_TILE // tb)
    assert bsz % sb == 0
    return sb, tb, t // tb


def _mod_spec(sb, n_t, layer, m, width=D_MODEL, col=lambda j: 0):
    if sb == 1:
        return pl.BlockSpec((None, None, 1, 1, width), lambda i, j: (layer, m, i // n_t, 0, col(j)))
    return pl.BlockSpec((None, None, sb, width), lambda i, j: (layer, m, i, col(j)))


SLAB = 256


def _slabs(sb, tb):
    n = min(SLAB, sb * tb)
    assert (sb * tb) % n == 0 and (sb == 1 or n % tb == 0)
    return [(r0, n) for r0 in range(0, sb * tb, n)]


def _tok_rows(x_ref, sb, tb, r0, n):
    if sb == 1:
        return x_ref[0, r0:r0 + n, :]
    return x_ref[r0 // tb:(r0 + n) // tb].reshape(n, x_ref.shape[-1])


def _put_rows(o_ref, sb, tb, r0, n, val):
    if sb == 1:
        o_ref[0, r0:r0 + n, :] = val
    else:
        o_ref[r0 // tb:(r0 + n) // tb] = val.reshape(n // tb, tb, val.shape[-1])


def _seq_rows(m_ref, sb, tb, r0, n):
    if sb == 1:
        return m_ref[0]
    n_seq = n // tb
    seq, tok = _iota((n, n_seq), 1), _iota((n, n_seq), 0)
    expand = jnp.where(jnp.logical_and(tok >= seq * tb, tok < (seq + 1) * tb), 1.0, 0.0)
    return _dot01(expand, m_ref[r0 // tb:(r0 + n) // tb])


def _write_modulated_norm(x_ref, g_ref, shift_ref, scale_ref, h_ref):
    sb, tb, _ = x_ref.shape
    for r0, n in _slabs(sb, tb):
        h = _modulated_norm(_tok_rows(x_ref, sb, tb, r0, n), g_ref[...],
                            _seq_rows(shift_ref, sb, tb, r0, n), _seq_rows(scale_ref, sb, tb, r0, n))
        h_ref[r0:r0 + n, :] = h.astype(BF16)


def _ffn_kernel(x_ref, shift_ref, scale_ref, gate_ref, g_ref, w1a_ref, w1b_ref, w2_ref, gfin_ref,
                o_ref, h_ref, *, final_norm):
    j = pl.program_id(1)
    sb, tb, d = x_ref.shape

    @pl.when(j == 0)
    def _():
        _write_modulated_norm(x_ref, g_ref, shift_ref, scale_ref, h_ref)
        o_ref[...] = jnp.zeros_like(o_ref)

    h = h_ref[...]
    u1 = jnp.dot(h, w1a_ref[...].astype(BF16), preferred_element_type=F32)
    u2 = jnp.dot(h, w1b_ref[...].astype(BF16), preferred_element_type=F32)
    act = (u1 * jax.nn.sigmoid(u1)) * u2
    o_ref[...] += _dot(act, w2_ref[...]).reshape(sb, tb, d)

    @pl.when(j == pl.num_programs(1) - 1)
    def _():
        for r0, n in _slabs(sb, tb):
            y = (_tok_rows(x_ref, sb, tb, r0, n)
                 + 0.5 * _seq_rows(gate_ref, sb, tb, r0, n) * _tok_rows(o_ref, sb, tb, r0, n))
            if final_norm:
                y = y * lax.rsqrt(jnp.mean(y * y, axis=-1, keepdims=True) + NORM_EPS) * gfin_ref[...]
            _put_rows(o_ref, sb, tb, r0, n, y)


def _ffn(x, mods, layer, m0, g, w_in, w_out, g_final, final_norm):
    bsz, t, d = x.shape
    sb, tb, n_t = _row_tiling(bsz, t)
    tf = 256
    nf = D_FF // tf
    x_spec = pl.BlockSpec((sb, tb, d), lambda i, j: (i // n_t, i % n_t, 0))
    return pl.pallas_call(
        functools.partial(_ffn_kernel, final_norm=final_norm),
        grid=((bsz // sb) * n_t, nf),
        in_specs=[
            x_spec,
            _mod_spec(sb, n_t, layer, m0), _mod_spec(sb, n_t, layer, m0 + 1),
            _mod_spec(sb, n_t, layer, m0 + 2),
            pl.BlockSpec((None, 1, d), lambda i, j: (layer, 0, 0)),
            pl.BlockSpec((None, d, tf), lambda i, j: (layer, 0, j)),
            pl.BlockSpec((None, d, tf), lambda i, j: (layer, 0, j + nf)),
            pl.BlockSpec((None, tf, d), lambda i, j: (layer, j, 0)),
            pl.BlockSpec((1, d), lambda i, j: (0, 0)),
        ],
        out_specs=x_spec,
        out_shape=jax.ShapeDtypeStruct(x.shape, F32),
        scratch_shapes=[pltpu.VMEM((sb * tb, d), BF16)],
        compiler_params=pltpu.CompilerParams(
            dimension_semantics=("parallel", "arbitrary"), vmem_limit_bytes=VMEM_LIMIT_BYTES),
        name="ffn",
    )(x, mods, mods, mods, g, w_in, w_in, w_out, g_final)


def _inproj_kernel(x_ref, shift_ref, scale_ref, g_ref, wt_ref, wtail_ref, og_ref, or_ref, h_ref):
    j = pl.program_id(1)
    shared = SEG // PROJ_TILE - 1
    last = N_PROJ_TILES - 1

    @pl.when(j == 0)
    def _():
        _write_modulated_norm(x_ref, g_ref, shift_ref, scale_ref, h_ref)

    def tile(w_ref):
        return lax.dot_general(h_ref[...], w_ref[...].astype(BF16), (((1,), (1,)), ((), ())),
                               preferred_element_type=F32)

    @pl.when(j < shared)
    def _():
        og_ref[...] = tile(wt_ref).reshape(og_ref.shape)

    @pl.when(j == shared)
    def _():
        both = tile(wt_ref)
        og_ref[...] = both.reshape(og_ref.shape)
        or_ref[...] = both.reshape(or_ref.shape)

    @pl.when(jnp.logical_and(j > shared, j < last))
    def _():
        or_ref[...] = tile(wt_ref).reshape(or_ref.shape)

    @pl.when(j == last)
    def _():
        or_ref[...] = tile(wtail_ref).reshape(or_ref.shape)


def _inproj(x, mods, layer, g, w_in_t, w_tail):
    bsz, t, d = x.shape
    sb, tb, n_t = _row_tiling(bsz, t, INPROJ_ROW_TILE)
    shared = SEG // PROJ_TILE - 1
    last = N_PROJ_TILES - 1
    row_map = lambda i: (i // n_t, i % n_t)
    return pl.pallas_call(
        _inproj_kernel,
        grid=((bsz // sb) * n_t, N_PROJ_TILES),
        in_specs=[
            pl.BlockSpec((sb, tb, d), lambda i, j: row_map(i) + (0,),
                         pipeline_mode=pl.Buffered(1)),
            _mod_spec(sb, n_t, layer, 3), _mod_spec(sb, n_t, layer, 4),
            pl.BlockSpec((None, 1, d), lambda i, j: (layer, 0, 0)),
            pl.BlockSpec((None, PROJ_TILE, d), lambda i, j: (layer, jnp.minimum(j, last - 1), 0)),
            pl.BlockSpec((None, PROJ_TILE, d), lambda i, j: (layer, 0, 0), pipeline_mode=pl.Buffered(1)),
        ],
        out_specs=[
            pl.BlockSpec((sb, tb, PROJ_TILE), lambda i, j: row_map(i) + (jnp.minimum(j, shared),)),
            pl.BlockSpec((sb, tb, PROJ_TILE), lambda i, j: row_map(i) + (jnp.maximum(j - shared, 0),)),
        ],
        out_shape=[jax.ShapeDtypeStruct((bsz, t, SEG), F32), jax.ShapeDtypeStruct((bsz, t, SEG), F32)],
        scratch_shapes=[pltpu.VMEM((sb * tb, d), BF16)],
        compiler_params=pltpu.CompilerParams(
            dimension_semantics=("parallel", "arbitrary"), vmem_limit_bytes=VMEM_LIMIT_BYTES),
        name="inproj",
    )(x, mods, mods, g, w_in_t, w_tail)


def _outproj_kernel(x_ref, gate_ref, og_ref, ow_ref, wg_ref, ww_ref, o_ref):
    sb, tb, tn = x_ref.shape
    og = og_ref[...].reshape(sb * tb, GLA_WIDTH)
    ow = ow_ref[...].reshape(sb * tb, RWKV_WIDTH)
    y = (jnp.dot(og, wg_ref[...].astype(BF16), preferred_element_type=F32)
         + jnp.dot(ow, ww_ref[...].astype(BF16), preferred_element_type=F32))
    o_ref[...] = (x_ref[...].reshape(sb * tb, tn)
                  + _seq_rows(gate_ref, sb, tb, 0, sb * tb) * y).reshape(sb, tb, tn)


def _outproj(x, mods, layer, o_gla, o_w, w_out):
    bsz, t, d = x.shape
    sb, tb, n_t = _row_tiling(bsz, t)
    tn = 1024
    x_spec = pl.BlockSpec((sb, tb, tn), lambda i, j: (i // n_t, i % n_t, j))
    o_spec = pl.BlockSpec((sb, tb, GLA_WIDTH), lambda i, j: (i // n_t, i % n_t, 0))
    return pl.pallas_call(
        _outproj_kernel,
        grid=((bsz // sb) * n_t, d // tn),
        in_specs=[
            x_spec,
            _mod_spec(sb, n_t, layer, 5, width=tn, col=lambda j: j),
            o_spec, o_spec,
            pl.BlockSpec((None, GLA_WIDTH, tn), lambda i, j: (layer, 0, j)),
            pl.BlockSpec((None, RWKV_WIDTH, tn), lambda i, j: (layer, 1, j)),
        ],
        out_specs=x_spec,
        out_shape=jax.ShapeDtypeStruct(x.shape, F32),
        compiler_params=pltpu.CompilerParams(
            dimension_semantics=("parallel", "arbitrary"), vmem_limit_bytes=VMEM_LIMIT_BYTES),
        name="outproj",
    )(x, mods, o_gla, o_w, w_out, w_out)


def _gla_kernel(pg_ref, s0_ref, up_ref, gb_ref, ng_ref, stack_ref, o_ref, s_ref, *, zero_init, sub):
    del stack_ref
    c_idx = pl.program_id(1)
    nb, c, _ = pg_ref.shape
    rows = nb * c
    log_c = c.bit_length() - 1
    n_sub = c // sub

    @pl.when(c_idx == 0)
    def _():
        if zero_init:
            s_ref[...] = jnp.zeros_like(s_ref)
        else:
            s_ref[...] = s0_ref[...]

    x_a = pg_ref[:, :, PG_A0:PG_A0 + LANES].reshape(rows, LANES)
    z = _dot(x_a, up_ref[...]) + gb_ref[...]
    gate_w = GLA_WIDTH + LANES
    gates = pltpu.roll(pg_ref[:, :, PG_A0:PG_A0 + gate_w].reshape(rows, gate_w),
                       gate_w - GLA_GATE_RANK, 1)
    glog = -_softplus(-z) / GLA_GATE_NORMALIZER
    r_i, c_i = _iota((rows, rows), 0), _iota((rows, rows), 1)
    same_seq = lax.shift_right_logical(r_i, log_c) == lax.shift_right_logical(c_i, log_c)
    tril = jnp.where(jnp.logical_and(same_seq, c_i <= r_i), 1.0, 0.0)
    b_all = _dot01(tril, glog)
    causal_ss = _iota((sub, sub), 1) <= _iota((sub, sub), 0)
    lane_ss = _iota((sub, sub), 1)

    grp = [(s, h) for s in range(nb) for h in range(GLA_HEADS)]
    kcols = lambda base, h: slice(base + h * GLA_HEAD_K, base + (h + 1) * GLA_HEAD_K)
    vcols = lambda base, h: slice(base + h * GLA_HEAD_V, base + (h + 1) * GLA_HEAD_V)
    q = jnp.stack([pg_ref[s, :, kcols(PG_Q0, h)] for s, h in grp]) * (GLA_HEAD_K ** -0.5)
    k = jnp.stack([pg_ref[s, :, kcols(PG_K0, h)] for s, h in grp])
    v = jnp.stack([pg_ref[s, :, vcols(PG_V0, h)] for s, h in grp])
    gate = jnp.stack([gates[s * c:(s + 1) * c, vcols(0, h)] for s, h in grp])
    b = jnp.stack([b_all[s * c:(s + 1) * c, kcols(0, h)] for s, h in grp])
    s0 = s_ref[...].reshape(len(grp), GLA_HEAD_K, GLA_HEAD_V)

    o_inter = _bdot(q * jnp.exp(b), s0)
    o_parts = []
    for i_sub in range(n_sub):
        r0 = i_sub * sub
        q_i, b_i = q[:, r0:r0 + sub], b[:, r0:r0 + sub]
        diag = jnp.zeros((len(grp), sub, sub), F32)
        b2_i = b_i * LOG2_E
        for jj in range(sub):
            dec = jnp.exp2(jnp.minimum(b2_i - b2_i[:, jj:jj + 1], 0.0))
            col = jnp.sum(q_i * dec * k[:, r0 + jj:r0 + jj + 1], axis=-1, keepdims=True)
            diag = jnp.where(lane_ss == jj, col, diag)
        o_i = _bdot(jnp.where(causal_ss, diag, 0.0), v[:, r0:r0 + sub])
        if i_sub > 0:
            b_start = b[:, r0 - 1:r0]
            q_rel = q_i * jnp.exp(b_i - b_start)
            k_rel = k[:, :r0] * jnp.exp(b_start - b[:, :r0])
            o_i = o_i + _bdot(_bdot_nt(q_rel, k_rel), v[:, :r0])
        o_parts.append(o_i)
    o = o_inter + (jnp.concatenate(o_parts, axis=1) if n_sub > 1 else o_parts[0])

    o = o * lax.rsqrt(jnp.mean(o * o, axis=-1, keepdims=True) + GLA_NORM_EPS) * ng_ref[...]
    o = (o * (gate * jax.nn.sigmoid(gate))).astype(o_ref.dtype)
    for g, (s, h) in enumerate(grp):
        o_ref[s, :, vcols(0, h)] = o[g]

    b_last = b[:, c - 1:c]
    k_bar = k * jnp.exp(b_last - b)
    decay_col = jnp.stack([
        jnp.concatenate([jnp.exp(jnp.broadcast_to(b_last[g], (GLA_HEAD_K, GLA_HEAD_K)).T)]
                        * (GLA_HEAD_V // GLA_HEAD_K), axis=1) for g in range(len(grp))])
    s_ref[...] = (s0 * decay_col + _bdot_tn(k_bar, v)).reshape(s_ref.shape)


def _stack_slice_spec(layer, blk):
    return pl.BlockSpec((None,) + blk, lambda b, c: (layer, b) + (0,) * (len(blk) - 1))


def _gla(proj, s0, up_p, gk_b, norm_g, layer, chunk, zero_init, stack):
    bsz, t, _ = proj.shape
    n_c = t // chunk
    nb = min(bsz, LANES // chunk, GLA_SEQ_PER_BLOCK)
    assert bsz % nb == 0
    if zero_init:
        s0_arg = jnp.zeros((nb, GLA_HEADS, GLA_HEAD_K, GLA_HEAD_V), F32)
        s0_spec = pl.BlockSpec((nb, GLA_HEADS, GLA_HEAD_K, GLA_HEAD_V), lambda b, c: (0, 0, 0, 0))
    else:
        s0_arg = s0
        s0_spec = pl.BlockSpec((None, nb, GLA_HEADS, GLA_HEAD_K, GLA_HEAD_V),
                               lambda b, c: (layer, b, 0, 0, 0))
    in_specs = [
        pl.BlockSpec((nb, chunk, SEG), lambda b, c: (b, c, 0)),
        s0_spec,
        pl.BlockSpec((None, LANES, GLA_KEY), lambda b, c: (layer, 0, 0)),
        pl.BlockSpec((None, 1, GLA_KEY), lambda b, c: (layer, 0, 0)),
        pl.BlockSpec((None, 1, GLA_HEAD_V), lambda b, c: (layer, 0, 0)),
        pl.BlockSpec(memory_space=pl.ANY),
    ]
    return pl.pallas_call(
        functools.partial(_gla_kernel, zero_init=zero_init, sub=min(GLA_SUB, chunk)),
        grid=(bsz // nb, n_c),
        in_specs=in_specs,
        out_specs=[
            pl.BlockSpec((nb, chunk, GLA_WIDTH), lambda b, c: (b, c, 0)),
            _stack_slice_spec(layer, (nb, GLA_HEADS, GLA_HEAD_K, GLA_HEAD_V)),
        ],
        out_shape=[
            jax.ShapeDtypeStruct((bsz, t, GLA_WIDTH), BF16),
            jax.ShapeDtypeStruct(stack.shape, F32),
        ],
        input_output_aliases={len(in_specs) - 1: 1},
        compiler_params=pltpu.CompilerParams(
            dimension_semantics=("parallel", "arbitrary"), vmem_limit_bytes=VMEM_LIMIT_BYTES),
        name="gla",
    )(proj, s0_arg, up_p, gk_b, norm_g, stack)


def _bdot(a, b):
    return lax.dot_general(a.astype(BF16), b.astype(BF16), (((2,), (1,)), ((0,), (0,))),
                           preferred_element_type=F32)


def _bdot_nt(a, b):
    return lax.dot_general(a.astype(BF16), b.astype(BF16), (((2,), (2,)), ((0,), (0,))),
                           preferred_element_type=F32)


def _bdot_tn(a, b):
    return lax.dot_general(a.astype(BF16), b.astype(BF16), (((1,), (1,)), ((0,), (0,))),
                           preferred_element_type=F32)


def _pair_tiles(s_heads):
    return jnp.where(_pair_mask(), jnp.concatenate([s_heads, s_heads], axis=-1), 0.0)


def _pair_heads(tiles):
    first = _iota((LANES, RWKV_HEAD), 0) < RWKV_HEAD
    return jnp.where(first, tiles[:, :, :RWKV_HEAD], tiles[:, :, RWKV_HEAD:])


def _pair_mask():
    return (_iota((LANES, LANES), 0) < RWKV_HEAD) == (_iota((LANES, LANES), 1) < RWKV_HEAD)


def _rwkv_kernel(pr_ref, sh0_ref, s0_ref, mu_ref, w0_ref, w2_ref, a0_ref, a2_ref, g2_ref, kk_ref,
                 ka_ref, rk_ref, lnw_ref, lnb_ref, stack_ref, o_ref, s_out_ref, sh_ref, s_ref, *, zero_init):
    del stack_ref
    c_idx = pl.program_id(1)
    nb, c, _ = pr_ref.shape
    rows = nb * c
    log_c = c.bit_length() - 1
    r2 = 2 * c
    g_all = nb * N_PAIR

    @pl.when(c_idx == 0)
    def _():
        sh_ref[...] = sh0_ref[...]
        if zero_init:
            s_ref[...] = jnp.zeros_like(s_ref)
        else:
            s_ref[...] = _pair_tiles(s0_ref[...].reshape(g_all, LANES, RWKV_HEAD)).reshape(s_ref.shape)

    pr = pltpu.roll(pr_ref[...].reshape(rows, SEG), SEG - PR_LEAD, 1)
    prev_rows = jnp.broadcast_to(sh_ref[...], (nb, c, SEG)).reshape(rows, SEG)
    first_tok = (_iota((rows, 1), 0) & (c - 1)) == 0
    pr_prev = jnp.where(first_tok, prev_rows, pltpu.roll(pr, 1, 0))
    sh_ref[...] = pr.reshape(nb, c, SEG)[:, c - 1:c, :]
    xr = pr + (pr_prev - pr) * mu_ref[...]

    r = xr[:, PR_R0:PR_R0 + RWKV_WIDTH]
    kr = xr[:, PR_K0:PR_K0 + RWKV_WIDTH]
    vr = xr[:, PR_V0:PR_V0 + RWKV_WIDTH]
    x_wa = xr[:, PR_WA0:PR_WA0 + LANES]
    x_g = xr[:, PR_G0:PR_G0 + PR_GW]
    w_inner = -_softplus(-(w0_ref[...] + _dot(jnp.tanh(x_wa), w2_ref[...]))) - 0.5
    log_w = -jnp.exp(w_inner)
    a = jax.nn.sigmoid(a0_ref[...] + _dot(x_wa, a2_ref[...]))
    gate = _dot(jax.nn.sigmoid(x_g), g2_ref[...])
    k_mod = kr * (1.0 + (a - 1.0) * ka_ref[...])
    kk_raw = kr * kk_ref[...]

    ones_blk = ((_iota((LANES, LANES), 0) < RWKV_HEAD) == (_iota((LANES, LANES), 1) < RWKV_HEAD)).astype(BF16)

    def head_sums(x):
        return jnp.concatenate([_segsum(x[:, j * LANES:(j + 1) * LANES], ones_blk)
                                for j in range(RWKV_WIDTH // LANES)], axis=1)

    kk = kk_raw / jnp.maximum(jnp.sqrt(head_sums(kk_raw * kk_raw)), 1e-12)
    beta = kk * a

    seq_r = lax.shift_right_logical(_iota((rows, rows), 0), log_c)
    seq_c = lax.shift_right_logical(_iota((rows, rows), 1), log_c)
    same_seq = seq_r == seq_c
    tril = jnp.where(jnp.logical_and(same_seq, _iota((rows, rows), 1) <= _iota((rows, rows), 0)), 1.0, 0.0)
    cum = _dot01(tril, log_w)
    cum_ex = cum - log_w
    cum_last = jnp.concatenate(
        [jnp.broadcast_to(cum[(s + 1) * c - 1:(s + 1) * c], (c, RWKV_WIDTH)) for s in range(nb)], axis=0)
    e_neg = jnp.exp(-cum)
    e_last = jnp.exp(cum_last - cum)

    head_mask = (_iota((r2, LANES), 0) < c) == (_iota((r2, LANES), 1) < RWKV_HEAD)

    def groups(m):
        parts = []
        for s in range(nb):
            for p in range(N_PAIR):
                blk = m[s * c:(s + 1) * c, p * LANES:(p + 1) * LANES]
                parts.append(jnp.where(head_mask, jnp.concatenate([blk, blk], axis=0), 0.0))
        return jnp.stack(parts, axis=0)

    al_s = groups(-kk * jnp.exp(cum_ex))
    rt_s = groups(r * jnp.exp(cum))
    bh_s = groups(beta * e_neg)
    kh_s = groups(k_mod * e_neg)
    v_s = groups(vr)
    bb_s = groups(beta * e_last)
    kb_s = groups(k_mod * e_last)

    row2 = _iota((r2, r2), 0)
    col2 = _iota((r2, r2), 1)
    same_head = (row2 < c) == (col2 < c)
    lower_strict = jnp.logical_and(same_head, col2 < row2)
    lower_incl = jnp.logical_and(same_head, col2 <= row2)
    eye2 = jnp.where(row2 == col2, 1.0, 0.0)

    if r2 % LANES == 0:
        quad = _bdot_nt(jnp.concatenate([al_s, rt_s], axis=1), jnp.concatenate([bh_s, kh_s], axis=1))
        a_b = jnp.where(lower_strict, quad[:, :r2, :r2], 0.0)
        a_k = jnp.where(lower_strict, quad[:, :r2, r2:], 0.0)
        b_bk = jnp.where(jnp.concatenate([lower_incl, lower_incl], axis=1), quad[:, r2:, :], 0.0)
        out_intra = lambda u: _bdot(b_bk, jnp.concatenate([u, v_s], axis=1))
    else:
        a_b = jnp.where(lower_strict, _bdot_nt(al_s, bh_s), 0.0)
        a_k = jnp.where(lower_strict, _bdot_nt(al_s, kh_s), 0.0)
        b_b = jnp.where(lower_incl, _bdot_nt(rt_s, bh_s), 0.0)
        b_k = jnp.where(lower_incl, _bdot_nt(rt_s, kh_s), 0.0)
        out_intra = lambda u: _bdot(b_b, u) + _bdot(b_k, v_s)
    t_inv = eye2 + a_b
    pw = a_b
    for _ in range(max(log_c - 1, 0)):
        pw = _bdot(pw, pw)
        t_inv = t_inv + _bdot(t_inv, pw)

    s0 = s_ref[...].reshape(g_all, LANES, LANES)
    u_s = _bdot(t_inv, _bdot(a_k, v_s) + _bdot_nt(al_s, s0))
    o_s = _bdot_nt(rt_s, s0) + out_intra(u_s)

    decay = jnp.stack([jnp.exp(cum_last[s * c:s * c + 1, p * LANES:(p + 1) * LANES])
                       for s in range(nb) for p in range(N_PAIR)], axis=0)
    s_new = s0 * decay + _bdot_tn(jnp.concatenate([u_s, v_s], axis=1), jnp.concatenate([bb_s, kb_s], axis=1))
    s_ref[...] = s_new.reshape(s_ref.shape)

    @pl.when(c_idx == pl.num_programs(1) - 1)
    def _():
        s_out_ref[...] = _pair_heads(s_new).reshape(s_out_ref.shape)

    o_g = o_s[:, :c, :] + o_s[:, c:, :]
    o2 = jnp.concatenate([jnp.concatenate([o_g[s * N_PAIR + p] for p in range(N_PAIR)], axis=1)
                          for s in range(nb)], axis=0)

    mean = head_sums(o2) * (1.0 / RWKV_HEAD)
    cen = o2 - mean
    var = head_sums(cen * cen) * (1.0 / RWKV_HEAD)
    y = cen * lax.rsqrt(var + RWKV_GN_EPS) * lnw_ref[...] + lnb_ref[...]
    y = y + head_sums(r * k_mod * rk_ref[...]) * vr
    o_ref[...] = (y * gate).reshape(nb, c, RWKV_WIDTH).astype(o_ref.dtype)


def _rwkv(proj, shift0, s0, wts, layer, chunk, zero_init, stack):
    bsz, t, _ = proj.shape
    n_c = t // chunk
    nb = min(bsz, LANES // chunk, RWKV_SEQ_PER_BLOCK)
    assert bsz % nb == 0 and LANES % chunk == 0
    head_blk = (nb, RWKV_HEADS, RWKV_HEAD, RWKV_HEAD)
    if zero_init:
        s0_arg = jnp.zeros(head_blk, F32)
        s0_spec = pl.BlockSpec(head_blk, lambda b, c: (0, 0, 0, 0))
    else:
        s0_arg = s0
        s0_spec = pl.BlockSpec((None,) + head_blk, lambda b, c: (layer, b, 0, 0, 0))
    vec = lambda n: pl.BlockSpec((None, 1, n), lambda b, c: (layer, 0, 0))
    mat = lambda m, n: pl.BlockSpec((None, m, n), lambda b, c: (layer, 0, 0))
    in_specs = [
        pl.BlockSpec((nb, chunk, SEG), lambda b, c: (b, c, 0)),
        pl.BlockSpec((None, nb, 1, SEG), lambda b, c: (layer, b, 0, 0)),
        s0_spec,
        vec(SEG), vec(RWKV_WIDTH), mat(LANES, RWKV_WIDTH), vec(RWKV_WIDTH),
        mat(LANES, RWKV_WIDTH), mat(PR_GW, RWKV_WIDTH), vec(RWKV_WIDTH), vec(RWKV_WIDTH),
        vec(RWKV_WIDTH), vec(RWKV_WIDTH), vec(RWKV_WIDTH),
        pl.BlockSpec(memory_space=pl.ANY),
    ]
    return pl.pallas_call(
        functools.partial(_rwkv_kernel, zero_init=zero_init),
        grid=(bsz // nb, n_c),
        in_specs=in_specs,
        out_specs=[
            pl.BlockSpec((nb, chunk, RWKV_WIDTH), lambda b, c: (b, c, 0)),
            _stack_slice_spec(layer, head_blk),
            pl.BlockSpec((nb, 1, SEG), lambda b, c: (b, 0, 0)),
        ],
        out_shape=[
            jax.ShapeDtypeStruct((bsz, t, RWKV_WIDTH), BF16),
            jax.ShapeDtypeStruct(stack.shape, F32),
            jax.ShapeDtypeStruct((bsz, 1, SEG), F32),
        ],
        input_output_aliases={len(in_specs) - 1: 1},
        scratch_shapes=[pltpu.VMEM((nb, N_PAIR, LANES, LANES), F32)],
        compiler_params=pltpu.CompilerParams(
            dimension_semantics=("parallel", "arbitrary"), vmem_limit_bytes=VMEM_LIMIT_BYTES),
        name="rwkv",
    )(proj, shift0, s0_arg, *wts, stack)


def _pad_rows(w, rows_before, rows_total):
    depth, r, n = w.shape
    return jnp.concatenate([jnp.zeros((depth, rows_before, n), w.dtype), w,
                            jnp.zeros((depth, rows_total - rows_before - r, n), w.dtype)], axis=1)


def _trunk(x, mods, state_gla, state_wkv, state_shift_p, params, zero_init):
    (g_ffn1, w_ffn1_in, w_ffn1_out, g_mix, w_in_t, w_tail, up_p, gk_b, norm_g, rwkv_wts, w_out, g_ffn2,
     w_ffn2_in, w_ffn2_out, g_final) = params
    depth = w_in_t.shape[0]
    bsz, t, _ = x.shape
    gla_chunk = min(GLA_CHUNK, t)
    rwkv_chunk = min(RWKV_CHUNK, t)
    new_gla = jnp.zeros((depth, bsz, GLA_HEADS, GLA_HEAD_K, GLA_HEAD_V), F32)
    new_wkv = jnp.zeros((depth, bsz, RWKV_HEADS, RWKV_HEAD, RWKV_HEAD), F32)
    new_shift = []
    for layer in range(depth):
        x = _ffn(x, mods, layer, 0, g_ffn1, w_ffn1_in, w_ffn1_out, g_final, False)
        proj_g, proj_r = _inproj(x, mods, layer, g_mix, w_in_t, w_tail)
        o_gla, new_gla = _gla(proj_g, state_gla, up_p, gk_b, norm_g, layer, gla_chunk, zero_init, new_gla)
        o_w, new_wkv, shift = _rwkv(proj_r, state_shift_p, state_wkv, rwkv_wts, layer, rwkv_chunk,
                                    zero_init, new_wkv)
        x = _outproj(x, mods, layer, o_gla, o_w, w_out)
        x = _ffn(x, mods, layer, 6, g_ffn2, w_ffn2_in, w_ffn2_out, g_final, layer == depth - 1)
        new_shift.append(shift[:, 0, :RWKV_IN])
    return x, new_gla, new_wkv, jnp.stack(new_shift)


def kernel(x_prompt, x_sample, c_prompt, c_sample, state_gla, state_wkv, state_shift, w_ada, b_ada, g_ffn1, w_ffn1_in, w_ffn1_out, g_mix, w_in, gla_gk_up, gla_gk_b, gla_norm_g, rwkv_mu, rwkv_w0, rwkv_w2, rwkv_a0, rwkv_a2, rwkv_g2, rwkv_k_k, rwkv_k_a, rwkv_r_k, rwkv_ln_w, rwkv_ln_b, w_out, g_ffn2, w_ffn2_in, w_ffn2_out, g_final):
    depth = w_in.shape[0]
    bp, bs = x_prompt.shape[0], x_sample.shape[0]
    assert x_prompt.shape[-1] == D_MODEL and w_in.shape[-1] == GLA_IN + RWKV_IN

    n_seq = bp + bs
    n_seq_p = -(-n_seq // SUBLANES) * SUBLANES
    c_all = jnp.concatenate([c_sample, c_prompt, jnp.zeros((n_seq_p - n_seq, D_MODEL), F32)], axis=0)
    mods = _adaln(c_all, w_ada, b_ada)
    mods_p = mods[:, :, bs:n_seq, None, :]

    w_in_t = jnp.swapaxes(w_in, 1, 2)
    tail0 = (N_PROJ_TILES - 1) * PROJ_TILE
    w_tail = jnp.pad(w_in_t[:, tail0:, :], ((0, 0), (0, N_PROJ_TILES * PROJ_TILE - w_in_t.shape[1]), (0, 0)))
    row = lambda v: v.reshape(depth, 1, -1)
    pad_seg = lambda v: jnp.pad(v, [(0, 0)] * (v.ndim - 1) + [(0, SEG - RWKV_IN)])
    up_p = _pad_rows(gla_gk_up, 0, LANES)
    rwkv_wts = (
        row(pad_seg(rwkv_mu)), row(rwkv_w0), _pad_rows(rwkv_w2, 0, LANES), row(rwkv_a0),
        _pad_rows(rwkv_a2, RWKV_W_LORA, LANES), _pad_rows(rwkv_g2, 0, PR_GW), row(rwkv_k_k),
        row(rwkv_k_a), row(rwkv_r_k), row(rwkv_ln_w), row(rwkv_ln_b))
    params = (row(g_ffn1), w_ffn1_in, w_ffn1_out, row(g_mix), w_in_t, w_tail, up_p, row(gla_gk_b),
              row(gla_norm_g), rwkv_wts, w_out, row(g_ffn2), w_ffn2_in, w_ffn2_out,
              g_final.reshape(1, D_MODEL))

    shift_zero = jnp.zeros((depth, bp, 1, SEG), F32)
    y_p, gla_p, wkv_p, shift_p = _trunk(x_prompt, mods_p, None, None, shift_zero, params, True)
    shift_s0 = pad_seg(state_shift)[:, :, None, :]
    y_s, gla_s, wkv_s, shift_s = _trunk(x_sample, mods, state_gla, state_wkv, shift_s0, params, False)
    return (y_p, y_s, gla_p, wkv_p, shift_p, gla_s, wkv_s, shift_s)
```

```python
import functools

import jax
import jax.numpy as jnp
from jax import lax
from jax.experimental import pallas as pl
from jax.experimental.pallas import tpu as pltpu

F32 = jnp.float32
BF16 = jnp.bfloat16

D_MODEL = 2048
D_FF = 5632
N_MOD = 9
NORM_EPS = 1e-6
GLA_HEADS = 4
GLA_HEAD_K = 128
GLA_HEAD_V = 256
GLA_KEY = GLA_HEADS * GLA_HEAD_K
GLA_WIDTH = GLA_HEADS * GLA_HEAD_V
GLA_GATE_RANK = 16
GLA_GATE_NORMALIZER = 16.0
GLA_NORM_EPS = 1e-5
GLA_IN = 2 * GLA_KEY + 2 * GLA_WIDTH + GLA_GATE_RANK
RWKV_HEAD = 64
RWKV_HEADS = 16
RWKV_WIDTH = RWKV_HEADS * RWKV_HEAD
RWKV_W_LORA = 64
RWKV_A_LORA = 64
RWKV_G_LORA = 160
RWKV_IN = 3 * RWKV_WIDTH + RWKV_W_LORA + RWKV_A_LORA + RWKV_G_LORA
RWKV_GN_EPS = 64e-5

LANES = 128
SUBLANES = 8
VMEM_LIMIT_BYTES = 60 * 1024 * 1024

PROJ_TILE = 512
SEG = 7 * PROJ_TILE
N_PROJ_TILES = -(-(GLA_IN + RWKV_IN) // PROJ_TILE)
PR_WIN0 = (N_PROJ_TILES - 7) * PROJ_TILE
PR_LEAD = GLA_IN - PR_WIN0
PG_Q0, PG_K0, PG_V0 = 0, GLA_KEY, 2 * GLA_KEY
PG_A0 = PG_V0 + GLA_WIDTH
PR_R0, PR_K0, PR_V0, PR_WA0, PR_G0 = 0, 1024, 2048, 3072, 3200
PR_GW = 256
N_PAIR = RWKV_HEADS // 2

ROW_TILE = 1024
INPROJ_ROW_TILE = 1024
GLA_CHUNK = 64
GLA_SUB = 16
GLA_SEQ_PER_BLOCK = 8
RWKV_CHUNK = 64
RWKV_SEQ_PER_BLOCK = 8


def _dot(a, b):
    return jnp.dot(a.astype(BF16), b.astype(BF16), preferred_element_type=F32)


def _dot_nt(a, b):
    return lax.dot_general(a.astype(BF16), b.astype(BF16), (((1,), (1,)), ((), ())),
                           preferred_element_type=F32)


def _dot_tn(a, b):
    return lax.dot_general(a.astype(BF16), b.astype(BF16), (((0,), (0,)), ((), ())),
                           preferred_element_type=F32)


def _split3(x):
    hi = x.astype(BF16)
    r1 = x - hi.astype(F32)
    mid = r1.astype(BF16)
    lo = (r1 - mid.astype(F32)).astype(BF16)
    return hi, mid, lo


def _dot01(m01, x):
    m = m01.astype(BF16)
    hi, mid, lo = _split3(x)
    f = lambda t: jnp.dot(m, t, preferred_element_type=F32)
    return f(hi) + f(mid) + f(lo)


def _segsum(x, ones_blk):
    hi = x.astype(BF16)
    lo = (x - hi.astype(F32)).astype(BF16)
    return (jnp.dot(hi, ones_blk, preferred_element_type=F32)
            + jnp.dot(lo, ones_blk, preferred_element_type=F32))


def _iota(shape, dim):
    return lax.broadcasted_iota(jnp.int32, shape, dim)


def _softplus(z):
    return jnp.maximum(z, 0.0) + jnp.log(1.0 + jnp.exp(-jnp.abs(z)))


def _modulated_norm(x, g, shift, scale):
    y = x * lax.rsqrt(jnp.mean(x * x, axis=-1, keepdims=True) + NORM_EPS) * g
    return y * (1.0 + scale) + shift


def _adaln_kernel(c_ref, w_ref, b_ref, o_ref):
    c = c_ref[...]
    o_ref[...] = _dot(c * jax.nn.sigmoid(c), w_ref[...]) + b_ref[...]


def _adaln(c_all, w_ada, b_ada):
    depth = w_ada.shape[0]
    rows = c_all.shape[0]
    tn = 1024
    nn = D_MODEL // tn
    return pl.pallas_call(
        _adaln_kernel,
        grid=(depth, N_MOD, nn),
        in_specs=[
            pl.BlockSpec((rows, D_MODEL), lambda l, m, n: (0, 0)),
            pl.BlockSpec((None, D_MODEL, tn), lambda l, m, n: (l, 0, m * nn + n)),
            pl.BlockSpec((None, 1, tn), lambda l, m, n: (l, 0, m * nn + n)),
        ],
        out_specs=pl.BlockSpec((None, None, rows, tn), lambda l, m, n: (l, m, 0, n)),
        out_shape=jax.ShapeDtypeStruct((depth, N_MOD, rows, D_MODEL), F32),
        compiler_params=pltpu.CompilerParams(
            dimension_semantics=("parallel", "parallel", "parallel"),
            vmem_limit_bytes=VMEM_LIMIT_BYTES),
        name="adaln",
    )(c_all, w_ada, b_ada.reshape(depth, 1, N_MOD * D_MODEL))


def _row_tiling(bsz, t, row_tile=ROW_TILE):
    tb = min(t, row_tile)
    assert t % tb == 0 and tb % SUBLANES == 0
    sb = 1 if tb > SUBLANES else min(bsz, ROW_TILE // tb)
    assert bsz % sb == 0
    return sb, tb, t // tb


def _mod_spec(sb, n_t, layer, m, width=D_MODEL, col=lambda j: 0):
    if sb == 1:
        return pl.BlockSpec((None, None, 1, 1, width), lambda i, j: (layer, m, i // n_t, 0, col(j)))
    return pl.BlockSpec((None, None, sb, width), lambda i, j: (layer, m, i, col(j)))


SLAB = 256


def _slabs(sb, tb):
    n = min(SLAB, sb * tb)
    assert (sb * tb) % n == 0 and (sb == 1 or n % tb == 0)
    return [(r0, n) for r0 in range(0, sb * tb, n)]


def _tok_rows(x_ref, sb, tb, r0, n):
    if sb == 1:
        return x_ref[0, r0:r0 + n, :]
    return x_ref[r0 // tb:(r0 + n) // tb].reshape(n, x_ref.shape[-1])


def _put_rows(o_ref, sb, tb, r0, n, val):
    if sb == 1:
        o_ref[0, r0:r0 + n, :] = val
    else:
        o_ref[r0 // tb:(r0 + n) // tb] = val.reshape(n // tb, tb, val.shape[-1])


def _seq_rows(m_ref, sb, tb, r0, n):
    if sb == 1:
        return m_ref[0]
    n_seq = n // tb
    seq, tok = _iota((n, n_seq), 1), _iota((n, n_seq), 0)
    expand = jnp.where(jnp.logical_and(tok >= seq * tb, tok < (seq + 1) * tb), 1.0, 0.0)
    return _dot01(expand, m_ref[r0 // tb:(r0 + n) // tb])


def _write_modulated_norm(x_ref, g_ref, shift_ref, scale_ref, h_ref):
    sb, tb, _ = x_ref.shape
    for r0, n in _slabs(sb, tb):
        h = _modulated_norm(_tok_rows(x_ref, sb, tb, r0, n), g_ref[...],
                            _seq_rows(shift_ref, sb, tb, r0, n), _seq_rows(scale_ref, sb, tb, r0, n))
        h_ref[r0:r0 + n, :] = h.astype(BF16)


def _ffn_kernel(x_ref, shift_ref, scale_ref, gate_ref, g_ref, w1a_ref, w1b_ref, w2_ref, gfin_ref,
                o_ref, h_ref, *, final_norm):
    j = pl.program_id(1)
    sb, tb, d = x_ref.shape

    @pl.when(j == 0)
    def _():
        _write_modulated_norm(x_ref, g_ref, shift_ref, scale_ref, h_ref)
        o_ref[...] = jnp.zeros_like(o_ref)

    h = h_ref[...]
    u1 = jnp.dot(h, w1a_ref[...].astype(BF16), preferred_element_type=F32)
    u2 = jnp.dot(h, w1b_ref[...].astype(BF16), preferred_element_type=F32)
    act = (u1 * jax.nn.sigmoid(u1)) * u2
    o_ref[...] += _dot(act, w2_ref[...]).reshape(sb, tb, d)

    @pl.when(j == pl.num_programs(1) - 1)
    def _():
        for r0, n in _slabs(sb, tb):
            y = (_tok_rows(x_ref, sb, tb, r0, n)
                 + 0.5 * _seq_rows(gate_ref, sb, tb, r0, n) * _tok_rows(o_ref, sb, tb, r0, n))
            if final_norm:
                y = y * lax.rsqrt(jnp.mean(y * y, axis=-1, keepdims=True) + NORM_EPS) * gfin_ref[...]
            _put_rows(o_ref, sb, tb, r0, n, y)


def _ffn(x, mods, layer, m0, g, w_in, w_out, g_final, final_norm):
    bsz, t, d = x.shape
    sb, tb, n_t = _row_tiling(bsz, t)
    tf = 256
    nf = D_FF // tf
    x_spec = pl.BlockSpec((sb, tb, d), lambda i, j: (i // n_t, i % n_t, 0))
    return pl.pallas_call(
        functools.partial(_ffn_kernel, final_norm=final_norm),
        grid=((bsz // sb) * n_t, nf),
        in_specs=[
            x_spec,
            _mod_spec(sb, n_t, layer, m0), _mod_spec(sb, n_t, layer, m0 + 1),
            _mod_spec(sb, n_t, layer, m0 + 2),
            pl.BlockSpec((None, 1, d), lambda i, j: (layer, 0, 0)),
            pl.BlockSpec((None, d, tf), lambda i, j: (layer, 0, j)),
            pl.BlockSpec((None, d, tf), lambda i, j: (layer, 0, j + nf)),
            pl.BlockSpec((None, tf, d), lambda i, j: (layer, j, 0)),
            pl.BlockSpec((1, d), lambda i, j: (0, 0)),
        ],
        out_specs=x_spec,
        out_shape=jax.ShapeDtypeStruct(x.shape, F32),
        scratch_shapes=[pltpu.VMEM((sb * tb, d), BF16)],
        compiler_params=pltpu.CompilerParams(
            dimension_semantics=("parallel", "arbitrary"), vmem_limit_bytes=VMEM_LIMIT_BYTES),
        name="ffn",
    )(x, mods, mods, mods, g, w_in, w_in, w_out, g_final)


def _inproj_kernel(x_ref, shift_ref, scale_ref, g_ref, wt_ref, wtail_ref, og_ref, or_ref, h_ref):
    j = pl.program_id(1)
    shared = SEG // PROJ_TILE - 1
    last = N_PROJ_TILES - 1

    @pl.when(j == 0)
    def _():
        _write_modulated_norm(x_ref, g_ref, shift_ref, scale_ref, h_ref)

    def tile(w_ref):
        return lax.dot_general(h_ref[...], w_ref[...].astype(BF16), (((1,), (1,)), ((), ())),
                               preferred_element_type=F32)

    @pl.when(j < shared)
    def _():
        og_ref[...] = tile(wt_ref).reshape(og_ref.shape)

    @pl.when(j == shared)
    def _():
        both = tile(wt_ref)
        og_ref[...] = both.reshape(og_ref.shape)
        or_ref[...] = both.reshape(or_ref.shape)

    @pl.when(jnp.logical_and(j > shared, j < last))
    def _():
        or_ref[...] = tile(wt_ref).reshape(or_ref.shape)

    @pl.when(j == last)
    def _():
        or_ref[...] = tile(wtail_ref).reshape(or_ref.shape)


def _inproj(x, mods, layer, g, w_in_t, w_tail):
    bsz, t, d = x.shape
    sb, tb, n_t = _row_tiling(bsz, t, INPROJ_ROW_TILE)
    shared = SEG // PROJ_TILE - 1
    last = N_PROJ_TILES - 1
    row_map = lambda i: (i // n_t, i % n_t)
    return pl.pallas_call(
        _inproj_kernel,
        grid=((bsz // sb) * n_t, N_PROJ_TILES),
        in_specs=[
            pl.BlockSpec((sb, tb, d), lambda i, j: row_map(i) + (0,),
                         pipeline_mode=pl.Buffered(1)),
            _mod_spec(sb, n_t, layer, 3), _mod_spec(sb, n_t, layer, 4),
            pl.BlockSpec((None, 1, d), lambda i, j: (layer, 0, 0)),
            pl.BlockSpec((None, PROJ_TILE, d), lambda i, j: (layer, jnp.minimum(j, last - 1), 0)),
            pl.BlockSpec((None, PROJ_TILE, d), lambda i, j: (layer, 0, 0), pipeline_mode=pl.Buffered(1)),
        ],
        out_specs=[
            pl.BlockSpec((sb, tb, PROJ_TILE), lambda i, j: row_map(i) + (jnp.minimum(j, shared),)),
            pl.BlockSpec((sb, tb, PROJ_TILE), lambda i, j: row_map(i) + (jnp.maximum(j - shared, 0),)),
        ],
        out_shape=[jax.ShapeDtypeStruct((bsz, t, SEG), F32), jax.ShapeDtypeStruct((bsz, t, SEG), F32)],
        scratch_shapes=[pltpu.VMEM((sb * tb, d), BF16)],
        compiler_params=pltpu.CompilerParams(
            dimension_semantics=("parallel", "arbitrary"), vmem_limit_bytes=VMEM_LIMIT_BYTES),
        name="inproj",
    )(x, mods, mods, g, w_in_t, w_tail)


def _outproj_kernel(x_ref, gate_ref, og_ref, ow_ref, wg_ref, ww_ref, o_ref):
    sb, tb, tn = x_ref.shape
    og = og_ref[...].reshape(sb * tb, GLA_WIDTH)
    ow = ow_ref[...].reshape(sb * tb, RWKV_WIDTH)
    y = (jnp.dot(og, wg_ref[...].astype(BF16), preferred_element_type=F32)
         + jnp.dot(ow, ww_ref[...].astype(BF16), preferred_element_type=F32))
    o_ref[...] = (x_ref[...].reshape(sb * tb, tn)
                  + _seq_rows(gate_ref, sb, tb, 0, sb * tb) * y).reshape(sb, tb, tn)


def _outproj(x, mods, layer, o_gla, o_w, w_out):
    bsz, t, d = x.shape
    sb, tb, n_t = _row_tiling(bsz, t)
    tn = 1024
    x_spec = pl.BlockSpec((sb, tb, tn), lambda i, j: (i // n_t, i % n_t, j))
    o_spec = pl.BlockSpec((sb, tb, GLA_WIDTH), lambda i, j: (i // n_t, i % n_t, 0))
    return pl.pallas_call(
        _outproj_kernel,
        grid=((bsz // sb) * n_t, d // tn),
        in_specs=[
            x_spec,
            _mod_spec(sb, n_t, layer, 5, width=tn, col=lambda j: j),
            o_spec, o_spec,
            pl.BlockSpec((None, GLA_WIDTH, tn), lambda i, j: (layer, 0, j)),
            pl.BlockSpec((None, RWKV_WIDTH, tn), lambda i, j: (layer, 1, j)),
        ],
        out_specs=x_spec,
        out_shape=jax.ShapeDtypeStruct(x.shape, F32),
        compiler_params=pltpu.CompilerParams(
            dimension_semantics=("parallel", "arbitrary"), vmem_limit_bytes=VMEM_LIMIT_BYTES),
        name="outproj",
    )(x, mods, o_gla, o_w, w_out, w_out)


def _gla_kernel(pg_ref, s0_ref, up_ref, gb_ref, ng_ref, stack_ref, o_ref, s_ref, *, zero_init, sub):
    del stack_ref
    c_idx = pl.program_id(1)
    nb, c, _ = pg_ref.shape
    rows = nb * c
    log_c = c.bit_length() - 1
    n_sub = c // sub

    @pl.when(c_idx == 0)
    def _():
        if zero_init:
            s_ref[...] = jnp.zeros_like(s_ref)
        else:
            s_ref[...] = s0_ref[...]

    x_a = pg_ref[:, :, PG_A0:PG_A0 + LANES].reshape(rows, LANES)
    z = _dot(x_a, up_ref[...]) + gb_ref[...]
    gate_w = GLA_WIDTH + LANES
    gates = pltpu.roll(pg_ref[:, :, PG_A0:PG_A0 + gate_w].reshape(rows, gate_w),
                       gate_w - GLA_GATE_RANK, 1)
    glog = -_softplus(-z) / GLA_GATE_NORMALIZER
    r_i, c_i = _iota((rows, rows), 0), _iota((rows, rows), 1)
    same_seq = lax.shift_right_logical(r_i, log_c) == lax.shift_right_logical(c_i, log_c)
    tril = jnp.where(jnp.logical_and(same_seq, c_i <= r_i), 1.0, 0.0)
    b_all = _dot01(tril, glog)
    causal_ss = _iota((sub, sub), 1) <= _iota((sub, sub), 0)
    lane_ss = _iota((sub, sub), 1)

    grp = [(s, h) for s in range(nb) for h in range(GLA_HEADS)]
    kcols = lambda base, h: slice(base + h * GLA_HEAD_K, base + (h + 1) * GLA_HEAD_K)
    vcols = lambda base, h: slice(base + h * GLA_HEAD_V, base + (h + 1) * GLA_HEAD_V)
    q = jnp.stack([pg_ref[s, :, kcols(PG_Q0, h)] for s, h in grp]) * (GLA_HEAD_K ** -0.5)
    k = jnp.stack([pg_ref[s, :, kcols(PG_K0, h)] for s, h in grp])
    v = jnp.stack([pg_ref[s, :, vcols(PG_V0, h)] for s, h in grp])
    gate = jnp.stack([gates[s * c:(s + 1) * c, vcols(0, h)] for s, h in grp])
    b = jnp.stack([b_all[s * c:(s + 1) * c, kcols(0, h)] for s, h in grp])
    s0 = s_ref[...].reshape(len(grp), GLA_HEAD_K, GLA_HEAD_V)

    o_inter = _bdot(q * jnp.exp(b), s0)
    o_parts = []
    for i_sub in range(n_sub):
        r0 = i_sub * sub
        q_i, b_i = q[:, r0:r0 + sub], b[:, r0:r0 + sub]
        diag = jnp.zeros((len(grp), sub, sub), F32)
        for jj in range(sub):
            dec = jnp.exp(jnp.minimum(b_i - b[:, r0 + jj:r0 + jj + 1], 0.0))
            col = jnp.sum(q_i * dec * k[:, r0 + jj:r0 + jj + 1], axis=-1, keepdims=True)
            diag = jnp.where(lane_ss == jj, col, diag)
        o_i = _bdot(jnp.where(causal_ss, diag, 0.0), v[:, r0:r0 + sub])
        if i_sub > 0:
            b_start = b[:, r0 - 1:r0]
            q_rel = q_i * jnp.exp(b_i - b_start)
            k_rel = k[:, :r0] * jnp.exp(b_start - b[:, :r0])
            o_i = o_i + _bdot(_bdot_nt(q_rel, k_rel), v[:, :r0])
        o_parts.append(o_i)
    o = o_inter + (jnp.concatenate(o_parts, axis=1) if n_sub > 1 else o_parts[0])

    o = o * lax.rsqrt(jnp.mean(o * o, axis=-1, keepdims=True) + GLA_NORM_EPS) * ng_ref[...]
    o = (o * (gate * jax.nn.sigmoid(gate))).astype(o_ref.dtype)
    for g, (s, h) in enumerate(grp):
        o_ref[s, :, vcols(0, h)] = o[g]

    b_last = b[:, c - 1:c]
    k_bar = k * jnp.exp(b_last - b)
    decay_col = jnp.stack([
        jnp.concatenate([jnp.exp(jnp.broadcast_to(b_last[g], (GLA_HEAD_K, GLA_HEAD_K)).T)]
                        * (GLA_HEAD_V // GLA_HEAD_K), axis=1) for g in range(len(grp))])
    s_ref[...] = (s0 * decay_col + _bdot_tn(k_bar, v)).reshape(s_ref.shape)


def _stack_slice_spec(layer, blk):
    return pl.BlockSpec((None,) + blk, lambda b, c: (layer, b) + (0,) * (len(blk) - 1))


def _gla(proj, s0, up_p, gk_b, norm_g, layer, chunk, zero_init, stack):
    bsz, t, _ = proj.shape
    n_c = t // chunk
    nb = min(bsz, LANES // chunk, GLA_SEQ_PER_BLOCK)
    assert bsz % nb == 0
    if zero_init:
        s0_arg = jnp.zeros((nb, GLA_HEADS, GLA_HEAD_K, GLA_HEAD_V), F32)
        s0_spec = pl.BlockSpec((nb, GLA_HEADS, GLA_HEAD_K, GLA_HEAD_V), lambda b, c: (0, 0, 0, 0))
    else:
        s0_arg = s0
        s0_spec = pl.BlockSpec((None, nb, GLA_HEADS, GLA_HEAD_K, GLA_HEAD_V),
                               lambda b, c: (layer, b, 0, 0, 0))
    in_specs = [
        pl.BlockSpec((nb, chunk, SEG), lambda b, c: (b, c, 0)),
        s0_spec,
        pl.BlockSpec((None, LANES, GLA_KEY), lambda b, c: (layer, 0, 0)),
        pl.BlockSpec((None, 1, GLA_KEY), lambda b, c: (layer, 0, 0)),
        pl.BlockSpec((None, 1, GLA_HEAD_V), lambda b, c: (layer, 0, 0)),
        pl.BlockSpec(memory_space=pl.ANY),
    ]
    return pl.pallas_call(
        functools.partial(_gla_kernel, zero_init=zero_init, sub=min(GLA_SUB, chunk)),
        grid=(bsz // nb, n_c),
        in_specs=in_specs,
        out_specs=[
            pl.BlockSpec((nb, chunk, GLA_WIDTH), lambda b, c: (b, c, 0)),
            _stack_slice_spec(layer, (nb, GLA_HEADS, GLA_HEAD_K, GLA_HEAD_V)),
        ],
        out_shape=[
            jax.ShapeDtypeStruct((bsz, t, GLA_WIDTH), BF16),
            jax.ShapeDtypeStruct(stack.shape, F32),
        ],
        input_output_aliases={len(in_specs) - 1: 1},
        compiler_params=pltpu.CompilerParams(
            dimension_semantics=("parallel", "arbitrary"), vmem_limit_bytes=VMEM_LIMIT_BYTES),
        name="gla",
    )(proj, s0_arg, up_p, gk_b, norm_g, stack)


def _bdot(a, b):
    return lax.dot_general(a.astype(BF16), b.astype(BF16), (((2,), (1,)), ((0,), (0,))),
                           preferred_element_type=F32)


def _bdot_nt(a, b):
    return lax.dot_general(a.astype(BF16), b.astype(BF16), (((2,), (2,)), ((0,), (0,))),
                           preferred_element_type=F32)


def _bdot_tn(a, b):
    return lax.dot_general(a.astype(BF16), b.astype(BF16), (((1,), (1,)), ((0,), (0,))),
                           preferred_element_type=F32)


def _pair_tiles(s_heads):
    return jnp.where(_pair_mask(), jnp.concatenate([s_heads, s_heads], axis=-1), 0.0)


def _pair_heads(tiles):
    first = _iota((LANES, RWKV_HEAD), 0) < RWKV_HEAD
    return jnp.where(first, tiles[:, :, :RWKV_HEAD], tiles[:, :, RWKV_HEAD:])


def _pair_mask():
    return (_iota((LANES, LANES), 0) < RWKV_HEAD) == (_iota((LANES, LANES), 1) < RWKV_HEAD)


def _rwkv_kernel(pr_ref, sh0_ref, s0_ref, mu_ref, w0_ref, w2_ref, a0_ref, a2_ref, g2_ref, kk_ref,
                 ka_ref, rk_ref, lnw_ref, lnb_ref, stack_ref, o_ref, s_out_ref, sh_ref, s_ref, *, zero_init):
    del stack_ref
    c_idx = pl.program_id(1)
    nb, c, _ = pr_ref.shape
    rows = nb * c
    log_c = c.bit_length() - 1
    r2 = 2 * c
    g_all = nb * N_PAIR

    @pl.when(c_idx == 0)
    def _():
        sh_ref[...] = sh0_ref[...]
        if zero_init:
            s_ref[...] = jnp.zeros_like(s_ref)
        else:
            s_ref[...] = _pair_tiles(s0_ref[...].reshape(g_all, LANES, RWKV_HEAD)).reshape(s_ref.shape)

    pr = pltpu.roll(pr_ref[...].reshape(rows, SEG), SEG - PR_LEAD, 1)
    prev_rows = jnp.broadcast_to(sh_ref[...], (nb, c, SEG)).reshape(rows, SEG)
    first_tok = (_iota((rows, 1), 0) & (c - 1)) == 0
    pr_prev = jnp.where(first_tok, prev_rows, pltpu.roll(pr, 1, 0))
    sh_ref[...] = pr.reshape(nb, c, SEG)[:, c - 1:c, :]
    xr = pr + (pr_prev - pr) * mu_ref[...]

    r = xr[:, PR_R0:PR_R0 + RWKV_WIDTH]
    kr = xr[:, PR_K0:PR_K0 + RWKV_WIDTH]
    vr = xr[:, PR_V0:PR_V0 + RWKV_WIDTH]
    x_wa = xr[:, PR_WA0:PR_WA0 + LANES]
    x_g = xr[:, PR_G0:PR_G0 + PR_GW]
    w_inner = -_softplus(-(w0_ref[...] + _dot(jnp.tanh(x_wa), w2_ref[...]))) - 0.5
    log_w = -jnp.exp(w_inner)
    a = jax.nn.sigmoid(a0_ref[...] + _dot(x_wa, a2_ref[...]))
    gate = _dot(jax.nn.sigmoid(x_g), g2_ref[...])
    k_mod = kr * (1.0 + (a - 1.0) * ka_ref[...])
    kk_raw = kr * kk_ref[...]

    ones_blk = ((_iota((LANES, LANES), 0) < RWKV_HEAD) == (_iota((LANES, LANES), 1) < RWKV_HEAD)).astype(BF16)

    def head_sums(x):
        return jnp.concatenate([_segsum(x[:, j * LANES:(j + 1) * LANES], ones_blk)
                                for j in range(RWKV_WIDTH // LANES)], axis=1)

    kk = kk_raw / jnp.maximum(jnp.sqrt(head_sums(kk_raw * kk_raw)), 1e-12)
    beta = kk * a

    seq_r = lax.shift_right_logical(_iota((rows, rows), 0), log_c)
    seq_c = lax.shift_right_logical(_iota((rows, rows), 1), log_c)
    same_seq = seq_r == seq_c
    tril = jnp.where(jnp.logical_and(same_seq, _iota((rows, rows), 1) <= _iota((rows, rows), 0)), 1.0, 0.0)
    cum = _dot01(tril, log_w)
    cum_ex = cum - log_w
    cum_last = jnp.concatenate(
        [jnp.broadcast_to(cum[(s + 1) * c - 1:(s + 1) * c], (c, RWKV_WIDTH)) for s in range(nb)], axis=0)
    e_neg = jnp.exp(-cum)
    e_last = jnp.exp(cum_last - cum)

    head_mask = (_iota((r2, LANES), 0) < c) == (_iota((r2, LANES), 1) < RWKV_HEAD)

    def groups(m):
        parts = []
        for s in range(nb):
            for p in range(N_PAIR):
                blk = m[s * c:(s + 1) * c, p * LANES:(p + 1) * LANES]
                parts.append(jnp.where(head_mask, jnp.concatenate([blk, blk], axis=0), 0.0))
        return jnp.stack(parts, axis=0)

    al_s = groups(-kk * jnp.exp(cum_ex))
    rt_s = groups(r * jnp.exp(cum))
    bh_s = groups(beta * e_neg)
    kh_s = groups(k_mod * e_neg)
    v_s = groups(vr)
    bb_s = groups(beta * e_last)
    kb_s = groups(k_mod * e_last)

    row2 = _iota((r2, r2), 0)
    col2 = _iota((r2, r2), 1)
    same_head = (row2 < c) == (col2 < c)
    lower_strict = jnp.logical_and(same_head, col2 < row2)
    lower_incl = jnp.logical_and(same_head, col2 <= row2)
    eye2 = jnp.where(row2 == col2, 1.0, 0.0)

    if r2 % LANES == 0:
        quad = _bdot_nt(jnp.concatenate([al_s, rt_s], axis=1), jnp.concatenate([bh_s, kh_s], axis=1))
        a_b = jnp.where(lower_strict, quad[:, :r2, :r2], 0.0)
        a_k = jnp.where(lower_strict, quad[:, :r2, r2:], 0.0)
        b_bk = jnp.where(jnp.concatenate([lower_incl, lower_incl], axis=1), quad[:, r2:, :], 0.0)
        out_intra = lambda u: _bdot(b_bk, jnp.concatenate([u, v_s], axis=1))
    else:
        a_b = jnp.where(lower_strict, _bdot_nt(al_s, bh_s), 0.0)
        a_k = jnp.where(lower_strict, _bdot_nt(al_s, kh_s), 0.0)
        b_b = jnp.where(lower_incl, _bdot_nt(rt_s, bh_s), 0.0)
        b_k = jnp.where(lower_incl, _bdot_nt(rt_s, kh_s), 0.0)
        out_intra = lambda u: _bdot(b_b, u) + _bdot(b_k, v_s)
    t_inv = eye2 + a_b
    pw = a_b
    for _ in range(max(log_c - 1, 0)):
        pw = _bdot(pw, pw)
        t_inv = t_inv + _bdot(t_inv, pw)

    s0 = s_ref[...].reshape(g_all, LANES, LANES)
    u_s = _bdot(t_inv, _bdot(a_k, v_s) + _bdot_nt(al_s, s0))
    o_s = _bdot_nt(rt_s, s0) + out_intra(u_s)

    decay = jnp.stack([jnp.exp(cum_last[s * c:s * c + 1, p * LANES:(p + 1) * LANES])
                       for s in range(nb) for p in range(N_PAIR)], axis=0)
    s_new = s0 * decay + _bdot_tn(jnp.concatenate([u_s, v_s], axis=1), jnp.concatenate([bb_s, kb_s], axis=1))
    s_ref[...] = s_new.reshape(s_ref.shape)

    @pl.when(c_idx == pl.num_programs(1) - 1)
    def _():
        s_out_ref[...] = _pair_heads(s_new).reshape(s_out_ref.shape)

    o_g = o_s[:, :c, :] + o_s[:, c:, :]
    o2 = jnp.concatenate([jnp.concatenate([o_g[s * N_PAIR + p] for p in range(N_PAIR)], axis=1)
                          for s in range(nb)], axis=0)

    mean = head_sums(o2) * (1.0 / RWKV_HEAD)
    cen = o2 - mean
    var = head_sums(cen * cen) * (1.0 / RWKV_HEAD)
    y = cen * lax.rsqrt(var + RWKV_GN_EPS) * lnw_ref[...] + lnb_ref[...]
    y = y + head_sums(r * k_mod * rk_ref[...]) * vr
    o_ref[...] = (y * gate).reshape(nb, c, RWKV_WIDTH).astype(o_ref.dtype)


def _rwkv(proj, shift0, s0, wts, layer, chunk, zero_init, stack):
    bsz, t, _ = proj.shape
    n_c = t // chunk
    nb = min(bsz, LANES // chunk, RWKV_SEQ_PER_BLOCK)
    assert bsz % nb == 0 and LANES % chunk == 0
    head_blk = (nb, RWKV_HEADS, RWKV_HEAD, RWKV_HEAD)
    if zero_init:
        s0_arg = jnp.zeros(head_blk, F32)
        s0_spec = pl.BlockSpec(head_blk, lambda b, c: (0, 0, 0, 0))
    else:
        s0_arg = s0
        s0_spec = pl.BlockSpec((None,) + head_blk, lambda b, c: (layer, b, 0, 0, 0))
    vec = lambda n: pl.BlockSpec((None, 1, n), lambda b, c: (layer, 0, 0))
    mat = lambda m, n: pl.BlockSpec((None, m, n), lambda b, c: (layer, 0, 0))
    in_specs = [
        pl.BlockSpec((nb, chunk, SEG), lambda b, c: (b, c, 0)),
        pl.BlockSpec((None, nb, 1, SEG), lambda b, c: (layer, b, 0, 0)),
        s0_spec,
        vec(SEG), vec(RWKV_WIDTH), mat(LANES, RWKV_WIDTH), vec(RWKV_WIDTH),
        mat(LANES, RWKV_WIDTH), mat(PR_GW, RWKV_WIDTH), vec(RWKV_WIDTH), vec(RWKV_WIDTH),
        vec(RWKV_WIDTH), vec(RWKV_WIDTH), vec(RWKV_WIDTH),
        pl.BlockSpec(memory_space=pl.ANY),
    ]
    return pl.pallas_call(
        functools.partial(_rwkv_kernel, zero_init=zero_init),
        grid=(bsz // nb, n_c),
        in_specs=in_specs,
        out_specs=[
            pl.BlockSpec((nb, chunk, RWKV_WIDTH), lambda b, c: (b, c, 0)),
            _stack_slice_spec(layer, head_blk),
            pl.BlockSpec((nb, 1, SEG), lambda b, c: (b, 0, 0)),
        ],
        out_shape=[
            jax.ShapeDtypeStruct((bsz, t, RWKV_WIDTH), BF16),
            jax.ShapeDtypeStruct(stack.shape, F32),
            jax.ShapeDtypeStruct((bsz, 1, SEG), F32),
        ],
        input_output_aliases={len(in_specs) - 1: 1},
        scratch_shapes=[pltpu.VMEM((nb, N_PAIR, LANES, LANES), F32)],
        compiler_params=pltpu.CompilerParams(
            dimension_semantics=("parallel", "arbitrary"), vmem_limit_bytes=VMEM_LIMIT_BYTES),
        name="rwkv",
    )(proj, shift0, s0_arg, *wts, stack)


def _pad_rows(w, rows_before, rows_total):
    depth, r, n = w.shape
    return jnp.concatenate([jnp.zeros((depth, rows_before, n), w.dtype), w,
                            jnp.zeros((depth, rows_total - rows_before - r, n), w.dtype)], axis=1)


def _trunk(x, mods, state_gla, state_wkv, state_shift_p, params, zero_init):
    (g_ffn1, w_ffn1_in, w_ffn1_out, g_mix, w_in_t, w_tail, up_p, gk_b, norm_g, rwkv_wts, w_out, g_ffn2,
     w_ffn2_in, w_ffn2_out, g_final) = params
    depth = w_in_t.shape[0]
    bsz, t, _ = x.shape
    gla_chunk = min(GLA_CHUNK, t)
    rwkv_chunk = min(RWKV_CHUNK, t)
    new_gla = jnp.zeros((depth, bsz, GLA_HEADS, GLA_HEAD_K, GLA_HEAD_V), F32)
    new_wkv = jnp.zeros((depth, bsz, RWKV_HEADS, RWKV_HEAD, RWKV_HEAD), F32)
    new_shift = []
    for layer in range(depth):
        x = _ffn(x, mods, layer, 0, g_ffn1, w_ffn1_in, w_ffn1_out, g_final, False)
        proj_g, proj_r = _inproj(x, mods, layer, g_mix, w_in_t, w_tail)
        o_gla, new_gla = _gla(proj_g, state_gla, up_p, gk_b, norm_g, layer, gla_chunk, zero_init, new_gla)
        o_w, new_wkv, shift = _rwkv(proj_r, state_shift_p, state_wkv, rwkv_wts, layer, rwkv_chunk,
                                    zero_init, new_wkv)
        x = _outproj(x, mods, layer, o_gla, o_w, w_out)
        x = _ffn(x, mods, layer, 6, g_ffn2, w_ffn2_in, w_ffn2_out, g_final, layer == depth - 1)
        new_shift.append(shift[:, 0, :RWKV_IN])
    return x, new_gla, new_wkv, jnp.stack(new_shift)


def kernel(x_prompt, x_sample, c_prompt, c_sample, state_gla, state_wkv, state_shift, w_ada, b_ada, g_ffn1, w_ffn1_in, w_ffn1_out, g_mix, w_in, gla_gk_up, gla_gk_b, gla_norm_g, rwkv_mu, rwkv_w0, rwkv_w2, rwkv_a0, rwkv_a2, rwkv_g2, rwkv_k_k, rwkv_k_a, rwkv_r_k, rwkv_ln_w, rwkv_ln_b, w_out, g_ffn2, w_ffn2_in, w_ffn2_out, g_final):
    depth = w_in.shape[0]
    bp, bs = x_prompt.shape[0], x_sample.shape[0]
    assert x_prompt.shape[-1] == D_MODEL and w_in.shape[-1] == GLA_IN + RWKV_IN

    n_seq = bp + bs
    n_seq_p = -(-n_seq // SUBLANES) * SUBLANES
    c_all = jnp.concatenate([c_sample, c_prompt, jnp.zeros((n_seq_p - n_seq, D_MODEL), F32)], axis=0)
    mods = _adaln(c_all, w_ada, b_ada)
    mods_p = mods[:, :, bs:n_seq, None, :]

    w_in_t = jnp.swapaxes(w_in, 1, 2)
    tail0 = (N_PROJ_TILES - 1) * PROJ_TILE
    w_tail = jnp.pad(w_in_t[:, tail0:, :], ((0, 0), (0, N_PROJ_TILES * PROJ_TILE - w_in_t.shape[1]), (0, 0)))
    row = lambda v: v.reshape(depth, 1, -1)
    pad_seg = lambda v: jnp.pad(v, [(0, 0)] * (v.ndim - 1) + [(0, SEG - RWKV_IN)])
    up_p = _pad_rows(gla_gk_up, 0, LANES)
    rwkv_wts = (
        row(pad_seg(rwkv_mu)), row(rwkv_w0), _pad_rows(rwkv_w2, 0, LANES), row(rwkv_a0),
        _pad_rows(rwkv_a2, RWKV_W_LORA, LANES), _pad_rows(rwkv_g2, 0, PR_GW), row(rwkv_k_k),
        row(rwkv_k_a), row(rwkv_r_k), row(rwkv_ln_w), row(rwkv_ln_b))
    to_mxu = lambda w: w.astype(BF16)
    params = (row(g_ffn1), to_mxu(w_ffn1_in), to_mxu(w_ffn1_out), row(g_mix), w_in_t, w_tail, up_p,
              row(gla_gk_b), row(gla_norm_g), rwkv_wts, w_out, row(g_ffn2), to_mxu(w_ffn2_in),
              to_mxu(w_ffn2_out), g_final.reshape(1, D_MODEL))

    shift_zero = jnp.zeros((depth, bp, 1, SEG), F32)
    y_p, gla_p, wkv_p, shift_p = _trunk(x_prompt, mods_p, None, None, shift_zero, params, True)
    shift_s0 = pad_seg(state_shift)[:, :, None, :]
    y_s, gla_s, wkv_s, shift_s = _trunk(x_sample, mods, state_gla, state_wkv, shift_s0, params, False)
    return (y_p, y_s, gla_p, wkv_p, shift_p, gla_s, wkv_s, shift_s)
```
